```python
import math
import jax, jax.numpy as jnp
from jax import lax
import numpy as np

D_MODEL = 1024
BATCH = 8
SEQ = 4096
DEPTH = 4

N_MIXERS = 3
DA_HEADS = 8
DA_HEAD_DIM = 64
Q_BLOCK = 128
DIL_GROUPS = ((128, 1), (512, 4), (2048, 16))
DIL_HEADS = 8
DIL_HEAD_DIM = 128
DIL_BLOCK = 128
GLA_HEADS = 4
GLA_DK = D_MODEL // 2
GLA_DV = D_MODEL
GLA_GATE_RANK = 16
GLA_TAU = 16.0
GLA_CHUNK = 64
D_FF = 4 * D_MODEL
ALPHA = (2 * DEPTH) ** 0.25
BETA = (8 * DEPTH) ** -0.25
LN_EPS = 1e-5
RMS_EPS = 1e-6

kernel_name = "hybrid_diffattn_dilated_gla_deepnorm"


def layer_norm(x, g, b):
    xf = x.astype(jnp.float32)
    mu = jnp.mean(xf, axis=-1, keepdims=True)
    var = jnp.mean(jnp.square(xf - mu), axis=-1, keepdims=True)
    return ((xf - mu) * lax.rsqrt(var + LN_EPS) * g + b).astype(x.dtype)


def rms_norm(x, g):
    xf = x.astype(jnp.float32)
    return (xf * lax.rsqrt(jnp.mean(xf * xf, axis=-1, keepdims=True) + RMS_EPS) * g).astype(x.dtype)


def alibi_slopes(n_heads):
    return 2.0 ** (-8.0 * jnp.arange(1, n_heads + 1, dtype=jnp.float32) / n_heads)


def diff_lambda_init(layer_idx):
    return 0.8 - 0.6 * math.exp(-0.3 * layer_idx)


def diff_attention(x, w_in, lam_q1, lam_k1, lam_q2, lam_k2, subln_g, w_out, lambda_init):
    B, S, _ = x.shape
    H, d = DA_HEADS, DA_HEAD_DIM
    q, k, v = jnp.split(x @ w_in, 3, axis=-1)
    q = q.reshape(B, S, H, 2, d).transpose(0, 2, 3, 1, 4)
    k = k.reshape(B, S, H, 2, d).transpose(0, 2, 3, 1, 4)
    v = v.reshape(B, S, H, 2 * d).transpose(0, 2, 1, 3)
    lam = (jnp.exp(jnp.sum(lam_q1.astype(jnp.float32) * lam_k1.astype(jnp.float32)))
           - jnp.exp(jnp.sum(lam_q2.astype(jnp.float32) * lam_k2.astype(jnp.float32)))
           + lambda_init)
    slopes = alibi_slopes(H)
    scale = d ** -0.5
    outs = []
    for blk in range(S // Q_BLOCK):
        q0 = blk * Q_BLOCK
        kv_len = q0 + Q_BLOCK
        s = jnp.einsum('bhmqd,bhmkd->bhmqk', q[:, :, :, q0:kv_len],
                       k[:, :, :, :kv_len]).astype(jnp.float32) * scale
        dist = (q0 + jnp.arange(Q_BLOCK))[:, None] - jnp.arange(kv_len)[None, :]
        bias = -slopes[:, None, None] * dist.astype(jnp.float32)
        s = jnp.where(dist >= 0, s + bias[None, :, None], -jnp.inf)
        p = jax.nn.softmax(s, axis=-1)
        a = p[:, :, 0] - lam * p[:, :, 1]
        outs.append(jnp.einsum('bhqk,bhkd->bhqd', a.astype(v.dtype), v[:, :, :kv_len]))
    o = jnp.concatenate(outs, axis=2)
    o = rms_norm(o, subln_g) * (1.0 - lambda_init)
    return o.transpose(0, 2, 1, 3).reshape(B, S, H * 2 * d) @ w_out


def strided_window_attention(q, k, v, window, dil, slopes):
    B, S, H, dh = q.shape
    L = S // dil
    W = window // dil
    BLK = DIL_BLOCK
    nb = -(-L // BLK)
    Lp = nb * BLK

    def to_phase(t):
        t = t.reshape(B, L, dil, H, dh).transpose(0, 2, 3, 1, 4)
        return jnp.pad(t, ((0, 0), (0, 0), (0, 0), (0, Lp - L), (0, 0)))

    def with_prev(t):
        t = jnp.pad(t, ((0, 0), (0, 0), (0, 0), (BLK, 0), (0, 0)))
        prev = t[:, :, :, :Lp].reshape(B, dil, H, nb, BLK, dh)
        cur = t[:, :, :, BLK:].reshape(B, dil, H, nb, BLK, dh)
        return jnp.concatenate([prev, cur], axis=4)

    qb = to_phase(q).reshape(B, dil, H, nb, BLK, dh)
    kb = with_prev(to_phase(k))
    vb = with_prev(to_phase(v))
    s = jnp.einsum('brhnqd,brhnkd->brhnqk', qb, kb).astype(jnp.float32) * (dh ** -0.5)
    qi = jnp.arange(BLK)
    kj = jnp.arange(2 * BLK)
    dist = qi[:, None] + BLK - kj[None, :]
    kpos = jnp.arange(nb)[:, None] * BLK + kj[None, :] - BLK
    valid = ((dist >= 0) & (dist <= W))[None] & (kpos >= 0)[:, None, :]
    bias = -slopes[:, None, None] * (dist * dil).astype(jnp.float32)
    s = jnp.where(valid, s + bias[:, None], -jnp.inf)
    lse = jax.nn.logsumexp(s, axis=-1)
    p = jnp.exp(s - lse[..., None])
    o = jnp.einsum('brhnqk,brhnkd->brhnqd', p.astype(vb.dtype), vb)
    o = o.reshape(B, dil, H, Lp, dh)[:, :, :, :L].transpose(0, 3, 1, 2, 4).reshape(B, S, H, dh)
    lse = lse.reshape(B, dil, H, Lp)[..., :L].transpose(0, 3, 1, 2).reshape(B, S, H)
    return o, lse


def dilated_attention(x, w_in, w_out):
    B, S, _ = x.shape
    H, dh = DIL_HEADS, DIL_HEAD_DIM
    G = len(DIL_GROUPS)
    qkv = (x @ w_in).reshape(B, S, G, 3, H, dh)
    slopes = alibi_slopes(H)
    outs, lses = [], []
    for g, (window, dil) in enumerate(DIL_GROUPS):
        o, lse = strided_window_attention(qkv[:, :, g, 0], qkv[:, :, g, 1], qkv[:, :, g, 2],
                                          window, dil, slopes)
        outs.append(o)
        lses.append(lse)
    wts = jax.nn.softmax(jnp.stack(lses, axis=0), axis=0)
    o = jnp.sum(wts[..., None].astype(x.dtype) * jnp.stack(outs, axis=0), axis=0)
    return o.reshape(B, S, H * dh) @ w_out


def gla_attention(x, w_in, w_gate2, b_gate, gnorm_g, w_out):
    B, S, _ = x.shape
    H, C = GLA_HEADS, GLA_CHUNK
    dk, dv = GLA_DK // H, GLA_DV // H
    nc = S // C
    proj = x @ w_in
    q, k, v, r, g_low = jnp.split(
        proj, [GLA_DK, 2 * GLA_DK, 2 * GLA_DK + GLA_DV, 2 * GLA_DK + 2 * GLA_DV], axis=-1)
    log_a = jax.nn.log_sigmoid((g_low @ w_gate2 + b_gate).astype(jnp.float32)) / GLA_TAU

    def chunked(t, d):
        return t.astype(jnp.float32).reshape(B, nc, C, H, d).transpose(1, 0, 3, 2, 4)

    qc = chunked(q, dk) * (dk ** -0.5)
    kc = chunked(k, dk)
    vc = chunked(v, dv)
    bcum = jnp.cumsum(chunked(log_a, dk), axis=3)
    b_last = bcum[:, :, :, -1:]
    q_dec = qc * jnp.exp(bcum)
    k_intra = kc * jnp.exp(-bcum)
    k_state = kc * jnp.exp(b_last - bcum)
    causal = jnp.tril(jnp.ones((C, C), dtype=bool))
    s = jnp.where(causal, jnp.einsum('nbhqd,nbhkd->nbhqk', q_dec, k_intra), 0.0)
    o_intra = jnp.einsum('nbhqk,nbhkd->nbhqd', s, vc)

    def step(state, inp):
        q_c, k_c, v_c, decay = inp
        o = jnp.einsum('bhqd,bhde->bhqe', q_c, state)
        state = state * decay[:, :, 0, :, None] + jnp.einsum('bhkd,bhke->bhde', k_c, v_c)
        return state, o

    state0 = jnp.zeros((B, H, dk, dv), jnp.float32)
    _, o_inter = lax.scan(step, state0, (q_dec, k_state, vc, jnp.exp(b_last)))
    o = rms_norm(o_intra + o_inter, gnorm_g)
    o = o.transpose(1, 0, 3, 2, 4).reshape(B, S, H * dv).astype(x.dtype)
    o = o * jax.nn.silu(r)
    return o @ w_out


def squared_relu_mlp(x, w1, w2):
    return jnp.square(jax.nn.relu(x @ w1)) @ w2


def setup_inputs(seed: int = 0) -> dict:
    key = jax.random.key(seed)
    keys = iter(jax.random.split(key, 96))

    def dense(fan_in, fan_out, scale=1.0):
        return jax.random.normal(next(keys), (fan_in, fan_out), jnp.float32) * (scale * fan_in ** -0.5)

    def gain(n):
        return 1.0 + 0.02 * jax.random.normal(next(keys), (n,), jnp.float32)

    def small(n, scale=0.02):
        return scale * jax.random.normal(next(keys), (n,), jnp.float32)

    inputs = {"x": jax.random.normal(next(keys), (BATCH, SEQ, D_MODEL), jnp.float32)}
    for i in range(DEPTH):
        p = f"l{i}_"
        kind = i % N_MIXERS
        if kind == 0:
            inputs[p + "w_in"] = dense(D_MODEL, 3 * DA_HEADS * 2 * DA_HEAD_DIM)
            inputs[p + "lam_q1"] = small(DA_HEAD_DIM, 0.1)
            inputs[p + "lam_k1"] = small(DA_HEAD_DIM, 0.1)
            inputs[p + "lam_q2"] = small(DA_HEAD_DIM, 0.1)
            inputs[p + "lam_k2"] = small(DA_HEAD_DIM, 0.1)
            inputs[p + "subln_g"] = gain(2 * DA_HEAD_DIM)
            inputs[p + "w_out"] = dense(DA_HEADS * 2 * DA_HEAD_DIM, D_MODEL, BETA)
        elif kind == 1:
            inputs[p + "w_in"] = dense(D_MODEL, len(DIL_GROUPS) * 3 * DIL_HEADS * DIL_HEAD_DIM)
            inputs[p + "w_out"] = dense(DIL_HEADS * DIL_HEAD_DIM, D_MODEL, BETA)
        else:
            inputs[p + "w_in"] = dense(D_MODEL, 2 * GLA_DK + 2 * GLA_DV + GLA_GATE_RANK)
            inputs[p + "w_gate2"] = dense(GLA_GATE_RANK, GLA_DK)
            inputs[p + "b_gate"] = small(GLA_DK, 0.5)
            inputs[p + "gnorm_g"] = gain(GLA_DV // GLA_HEADS)
            inputs[p + "w_out"] = dense(GLA_DV, D_MODEL, BETA)
        inputs[p + "ln1_g"] = gain(D_MODEL)
        inputs[p + "ln1_b"] = small(D_MODEL)
        inputs[p + "w_ff1"] = dense(D_MODEL, D_FF)
        inputs[p + "w_ff2"] = dense(D_FF, D_MODEL, BETA)
        inputs[p + "ln2_g"] = gain(D_MODEL)
        inputs[p + "ln2_b"] = small(D_MODEL)
    return inputs


def reference(x,
              l0_w_in, l0_lam_q1, l0_lam_k1, l0_lam_q2, l0_lam_k2, l0_subln_g, l0_w_out,
              l0_ln1_g, l0_ln1_b, l0_w_ff1, l0_w_ff2, l0_ln2_g, l0_ln2_b,
              l1_w_in, l1_w_out,
              l1_ln1_g, l1_ln1_b, l1_w_ff1, l1_w_ff2, l1_ln2_g, l1_ln2_b,
              l2_w_in, l2_w_gate2, l2_b_gate, l2_gnorm_g, l2_w_out,
              l2_ln1_g, l2_ln1_b, l2_w_ff1, l2_w_ff2, l2_ln2_g, l2_ln2_b,
              l3_w_in, l3_lam_q1, l3_lam_k1, l3_lam_q2, l3_lam_k2, l3_subln_g, l3_w_out,
              l3_ln1_g, l3_ln1_b, l3_w_ff1, l3_w_ff2, l3_ln2_g, l3_ln2_b):
    layer_params = (
        ((l0_w_in, l0_lam_q1, l0_lam_k1, l0_lam_q2, l0_lam_k2, l0_subln_g, l0_w_out),
         (l0_ln1_g, l0_ln1_b, l0_w_ff1, l0_w_ff2, l0_ln2_g, l0_ln2_b)),
        ((l1_w_in, l1_w_out),
         (l1_ln1_g, l1_ln1_b, l1_w_ff1, l1_w_ff2, l1_ln2_g, l1_ln2_b)),
        ((l2_w_in, l2_w_gate2, l2_b_gate, l2_gnorm_g, l2_w_out),
         (l2_ln1_g, l2_ln1_b, l2_w_ff1, l2_w_ff2, l2_ln2_g, l2_ln2_b)),
        ((l3_w_in, l3_lam_q1, l3_lam_k1, l3_lam_q2, l3_lam_k2, l3_subln_g, l3_w_out),
         (l3_ln1_g, l3_ln1_b, l3_w_ff1, l3_w_ff2, l3_ln2_g, l3_ln2_b)),
    )
    for i in range(DEPTH):
        mix_p, (g1, b1, w1, w2, g2, b2) = layer_params[i]
        kind = i % N_MIXERS
        if kind == 0:
            y = diff_attention(x, *mix_p, lambda_init=diff_lambda_init(i))
        elif kind == 1:
            y = dilated_attention(x, *mix_p)
        else:
            y = gla_attention(x, *mix_p)
        x = layer_norm(ALPHA * x + y, g1, b1)
        x = layer_norm(ALPHA * x + squared_relu_mlp(x, w1, w2), g2, b2)
    return x
```

```python
import functools
import math

import jax
import jax.numpy as jnp
from jax import lax
from jax.experimental import pallas as pl
from jax.experimental.pallas import tpu as pltpu

F32 = jnp.float32
BF16 = jnp.bfloat16

D_MODEL = 1024
DEPTH = 4
N_MIXERS = 3
DA_HEADS = 8
DA_HEAD_DIM = 64
DIL_GROUPS = ((128, 1), (512, 4), (2048, 16))
DIL_HEADS = 8
DIL_HEAD_DIM = 128
DIL_BLOCK = 128
GLA_HEADS = 4
GLA_DK = D_MODEL // 2
GLA_DV = D_MODEL
GLA_GATE_RANK = 16
GLA_TAU = 16.0
GLA_CHUNK = 64
D_FF = 4 * D_MODEL
ALPHA = (2 * DEPTH) ** 0.25
LN_EPS = 1e-5
RMS_EPS = 1e-6

LANES = 128
VMEM_LIMIT = 56 * 1024 * 1024
NEG = -1e30


def _params(*sem):
    return pltpu.CompilerParams(dimension_semantics=sem, vmem_limit_bytes=VMEM_LIMIT)


def _layer_norm(z, g, b):
    mu = jnp.mean(z, axis=-1, keepdims=True)
    zc = z - mu
    var = jnp.mean(zc * zc, axis=-1, keepdims=True)
    return zc * lax.rsqrt(var + LN_EPS) * g + b


def _dot_nt(a, b):
    return lax.dot_general(a, b, (((1,), (1,)), ((), ())), preferred_element_type=F32)


def _proj_body(x_ref, w_ref, o_ref, xb_ref, *acc_scratch, dil, head_major):
    @pl.when(pl.program_id(2) == 0)
    def _():
        xb_ref[...] = x_ref[0].astype(BF16)

    acc = jnp.dot(xb_ref[...], w_ref[...], preferred_element_type=F32)
    tm, tn = acc.shape
    if not head_major:
        o_ref[0] = acc.astype(o_ref.dtype)
    elif dil == 1:
        for c in range(tn // LANES):
            o_ref[0, c, 0] = acc[:, c * LANES:(c + 1) * LANES].astype(o_ref.dtype)
    else:
        acc_ref, = acc_scratch
        rows = tm // dil
        for c in range(tn // LANES):
            acc_ref[c] = acc[:, c * LANES:(c + 1) * LANES]
            for r in range(dil):
                o_ref[0, c, r] = acc_ref[c, pl.ds(r, rows, stride=dil), :].astype(o_ref.dtype)


def _proj(x, w, *, dil=1, head_major=True, out_dtype=BF16, tm=1024, tn=1024):
    B, S, D = x.shape
    N = w.shape[1]
    tm = min(tm, S)
    tn = min(tn, N)
    assert S % tm == 0 and N % tn == 0 and tm % dil == 0 and tn % LANES == 0
    if head_major:
        out_shape = jax.ShapeDtypeStruct((B, N // LANES, dil, S // dil, LANES), out_dtype)
        out_spec = pl.BlockSpec((1, tn // LANES, dil, tm // dil, LANES),
                                lambda b, i, j: (b, j, 0, i, 0))
    else:
        out_shape = jax.ShapeDtypeStruct((B, S, N), out_dtype)
        out_spec = pl.BlockSpec((1, tm, tn), lambda b, i, j: (b, i, j))
    scratch = [pltpu.VMEM((tm, D), BF16)]
    if head_major and dil > 1:
        scratch.append(pltpu.VMEM((tn // LANES, tm, LANES), F32))
    return pl.pallas_call(
        functools.partial(_proj_body, dil=dil, head_major=head_major),
        grid=(B, S // tm, N // tn),
        in_specs=[pl.BlockSpec((1, tm, D), lambda b, i, j: (b, i, 0)),
                  pl.BlockSpec((D, tn), lambda b, i, j: (0, j))],
        out_specs=out_spec,
        out_shape=out_shape,
        scratch_shapes=scratch,
        compiler_params=_params("parallel", "parallel", "arbitrary"),
        name="proj_in",
    )(x, w)


def _out_ln_body(o_ref, w_ref, x_ref, g_ref, b_ref, y_ref, *, nblk):
    o = jnp.concatenate([o_ref[0, c] for c in range(nblk)], axis=-1)
    y = jnp.dot(o, w_ref[...], preferred_element_type=F32)
    y_ref[0] = _layer_norm(ALPHA * x_ref[0] + y, g_ref[...], b_ref[...])


def _out_ln(o, w, x, g, b, *, tm=512):
    B, nblk, S, wblk = o.shape
    D = x.shape[-1]
    tm = min(tm, S)
    return pl.pallas_call(
        functools.partial(_out_ln_body, nblk=nblk),
        grid=(B, S // tm),
        in_specs=[pl.BlockSpec((1, nblk, tm, wblk), lambda b, i: (b, 0, i, 0)),
                  pl.BlockSpec((nblk * wblk, D), lambda b, i: (0, 0)),
                  pl.BlockSpec((1, tm, D), lambda b, i: (b, i, 0)),
                  pl.BlockSpec((1, D), lambda b, i: (0, 0)),
                  pl.BlockSpec((1, D), lambda b, i: (0, 0))],
        out_specs=pl.BlockSpec((1, tm, D), lambda b, i: (b, i, 0)),
        out_shape=jax.ShapeDtypeStruct(x.shape, F32),
        compiler_params=_params("parallel", "parallel"),
        name="out_proj_ln",
    )(o, w, x, g.reshape(1, D), b.reshape(1, D))


def _mlp_body(x_ref, w1_ref, w2_ref, g_ref, b_ref, y_ref, xb_ref, acc_ref):
    k = pl.program_id(2)

    @pl.when(k == 0)
    def _():
        xb_ref[...] = x_ref[0].astype(BF16)
        acc_ref[...] = jnp.zeros_like(acc_ref)

    h = jnp.dot(xb_ref[...], w1_ref[...], preferred_element_type=F32)
    h = jnp.square(jnp.maximum(h, 0.0)).astype(BF16)
    acc_ref[...] += jnp.dot(h, w2_ref[...], preferred_element_type=F32)

    @pl.when(k == pl.num_programs(2) - 1)
    def _():
        y_ref[0] = _layer_norm(ALPHA * x_ref[0] + acc_ref[...], g_ref[...], b_ref[...])


def _mlp_ln(x, w1, w2, g, b, *, tm=1024, tf=512):
    B, S, D = x.shape
    FF = w1.shape[1]
    tm = min(tm, S)
    return pl.pallas_call(
        _mlp_body,
        grid=(B, S // tm, FF // tf),
        in_specs=[pl.BlockSpec((1, tm, D), lambda b, i, k: (b, i, 0)),
                  pl.BlockSpec((D, tf), lambda b, i, k: (0, k)),
                  pl.BlockSpec((tf, D), lambda b, i, k: (k, 0)),
                  pl.BlockSpec((1, D), lambda b, i, k: (0, 0)),
                  pl.BlockSpec((1, D), lambda b, i, k: (0, 0))],
        out_specs=pl.BlockSpec((1, tm, D), lambda b, i, k: (b, i, 0)),
        out_shape=jax.ShapeDtypeStruct(x.shape, F32),
        scratch_shapes=[pltpu.VMEM((tm, D), BF16), pltpu.VMEM((tm, D), F32)],
        compiler_params=_params("parallel", "parallel", "arbitrary"),
        name="mlp_ln",
    )(x, w1, w2, g.reshape(1, D), b.reshape(1, D))


def _diff_body(slopes_ref, q_ref, k_ref, v_ref, lq1_ref, lk1_ref, lq2_ref, lk2_ref, g_ref,
               o_ref, *, tq, tk, lambda_init):
    h = pl.program_id(1)
    i = pl.program_id(2)
    d = DA_HEAD_DIM
    slope = slopes_ref[h]
    scale = d ** -0.5
    q0 = i * tq

    q = q_ref[0, 0, 0]
    lane = lax.broadcasted_iota(jnp.int32, q.shape, 1)
    zero = jnp.zeros_like(q)
    q_maps = (jnp.where(lane < d, q, zero), jnp.where(lane >= d, q, zero))

    rel = (lax.broadcasted_iota(jnp.int32, (tq, tk), 1)
           - lax.broadcasted_iota(jnp.int32, (tq, tk), 0))
    bias0 = slope * rel.astype(F32)

    def step(j, carry, masked):
        k0 = pl.multiple_of(j * tk, tk)
        kt = k_ref[0, 0, 0, pl.ds(k0, tk), :]
        vt = v_ref[0, 0, 0, pl.ds(k0, tk), :]
        shift = k0 - q0
        bias = bias0 + slope * shift.astype(F32)
        new = []
        for mp in range(2):
            m, l, acc = carry[mp]
            s = _dot_nt(q_maps[mp], kt) * scale + bias
            if masked:
                s = jnp.where(rel + shift <= 0, s, NEG)
            m_new = jnp.maximum(m, jnp.max(s, axis=-1, keepdims=True))
            p = jnp.exp(s - m_new)
            a = jnp.exp(m - m_new)
            l = a * l + jnp.sum(p, axis=-1, keepdims=True)
            acc = a * acc + jnp.dot(p.astype(BF16), vt, preferred_element_type=F32)
            new.append((m_new, l, acc))
        return tuple(new)

    init_one = (jnp.full((tq, 1), NEG, F32), jnp.zeros((tq, 1), F32), jnp.zeros((tq, 2 * d), F32))
    carry = (init_one, init_one)
    n_full = q0 // tk
    n_tot = (q0 + tq + tk - 1) // tk
    carry = lax.fori_loop(0, n_full, functools.partial(step, masked=False), carry)
    carry = lax.fori_loop(n_full, n_tot, functools.partial(step, masked=True), carry)
    (_, l1, a1), (_, l2, a2) = carry

    lam = (jnp.exp(jnp.sum(lq1_ref[...] * lk1_ref[...], axis=-1, keepdims=True))
           - jnp.exp(jnp.sum(lq2_ref[...] * lk2_ref[...], axis=-1, keepdims=True))
           + lambda_init)
    o = a1 / l1 - lam * (a2 / l2)
    o = o * lax.rsqrt(jnp.mean(o * o, axis=-1, keepdims=True) + RMS_EPS) * g_ref[...]
    o_ref[0, 0] = (o * (1.0 - lambda_init)).astype(o_ref.dtype)


def _diff_attention(qkv, lq1, lk1, lq2, lk2, subln_g, lambda_init, *, tq=256, tk=256):
    B, _, _, S, _ = qkv.shape
    H, d = DA_HEADS, DA_HEAD_DIM
    tq, tk = min(tq, S), min(tk, S)
    slopes = 2.0 ** (-8.0 * jnp.arange(1, H + 1, dtype=F32) / H)
    vec = lambda a: a.reshape(1, -1).astype(F32)
    small = lambda n: pl.BlockSpec((1, n), lambda b, h, i: (0, 0))
    return pl.pallas_call(
        functools.partial(_diff_body, tq=tq, tk=tk, lambda_init=lambda_init),
        grid=(B, H, S // tq),
        in_specs=[pl.BlockSpec(memory_space=pltpu.SMEM),
                  pl.BlockSpec((1, 1, 1, tq, 2 * d), lambda b, h, i: (b, h, 0, i, 0)),
                  pl.BlockSpec((1, 1, 1, S, 2 * d), lambda b, h, i: (b, H + h, 0, 0, 0)),
                  pl.BlockSpec((1, 1, 1, S, 2 * d), lambda b, h, i: (b, 2 * H + h, 0, 0, 0)),
                  small(d), small(d), small(d), small(d), small(2 * d)],
        out_specs=pl.BlockSpec((1, 1, tq, 2 * d), lambda b, h, i: (b, h, i, 0)),
        out_shape=jax.ShapeDtypeStruct((B, H, S, 2 * d), BF16),
        compiler_params=_params("parallel", "parallel", "arbitrary"),
        name="diff_attn",
    )(slopes, qkv, qkv, qkv, vec(lq1), vec(lk1), vec(lq2), vec(lk2), vec(subln_g))


def _dil_body(slopes_ref, *refs, tt, dils):
    G = len(dils)
    in_refs, o_ref = refs[:3 * G], refs[3 * G]
    acc_refs = refs[3 * G + 1:3 * G + 1 + G]
    m_refs = refs[3 * G + 1 + G:3 * G + 1 + 2 * G]
    l_refs = refs[3 * G + 1 + 2 * G:3 * G + 1 + 3 * G]
    h = pl.program_id(1)
    t = pl.program_id(2)
    slope = slopes_ref[h]
    blk = DIL_BLOCK
    scale = DIL_HEAD_DIM ** -0.5
    base = (lax.broadcasted_iota(jnp.int32, (blk, 2 * blk), 0)
            - lax.broadcasted_iota(jnp.int32, (blk, 2 * blk), 1))

    for g in range(G):
        dil = dils[g]
        window = DIL_GROUPS[g][0] // dil
        q_ref, k_ref, v_ref = in_refs[3 * g:3 * g + 3]
        rows = tt // dil
        nb = rows // blk

        def block(idx, _, dil=dil, window=window, q_ref=q_ref, k_ref=k_ref, v_ref=v_ref,
                  rows=rows, nb=nb, g=g):
            r = idx // nb
            bi = idx % nb
            lq = pl.multiple_of(bi * blk, blk)
            l0 = t * rows + lq
            ks = pl.multiple_of(jnp.maximum(l0 - blk, 0), blk)
            qb = q_ref[0, 0, r, pl.ds(lq, blk), :]
            kc = k_ref[0, 0, r, pl.ds(ks, 2 * blk), :]
            vc = v_ref[0, 0, r, pl.ds(ks, 2 * blk), :]
            dist = base + (l0 - ks)
            s = _dot_nt(qb, kc) * scale - (slope * dil) * dist.astype(F32)
            s = jnp.where((dist >= 0) & (dist <= window), s, NEG)
            m = jnp.max(s, axis=-1, keepdims=True)
            p = jnp.exp(s - m)
            l = jnp.sum(p, axis=-1, keepdims=True)
            acc = jnp.dot(p.astype(BF16), vc, preferred_element_type=F32)
            tok = pl.ds(lq * dil + r, blk, stride=dil)
            acc_refs[g][tok, :] = acc
            m_refs[g][tok, :] = jnp.broadcast_to(m, (blk, LANES))
            l_refs[g][tok, :] = jnp.broadcast_to(l, (blk, LANES))
            return 0

        lax.fori_loop(0, dil * nb, block, 0)

    ch = 256
    for c in range(tt // ch):
        sl = slice(c * ch, (c + 1) * ch)
        ms = [m_refs[g][sl, :] for g in range(G)]
        top = functools.reduce(jnp.maximum, ms)
        ws = [jnp.exp(m - top) for m in ms]
        num = sum(w * acc_refs[g][sl, :] for g, w in enumerate(ws))
        den = sum(w * l_refs[g][sl, :] for g, w in enumerate(ws))
        o_ref[0, 0, sl, :] = (num / den).astype(o_ref.dtype)


def _dilated_attention(qkvs, *, tt=2048):
    dils = tuple(d for _, d in DIL_GROUPS)
    B = qkvs[0].shape[0]
    S = qkvs[0].shape[2] * qkvs[0].shape[3]
    H, dh = DIL_HEADS, DIL_HEAD_DIM
    tt = min(tt, S)
    assert all(tt % (d * DIL_BLOCK) == 0 and S // d >= 2 * DIL_BLOCK for d in dils)
    slopes = 2.0 ** (-8.0 * jnp.arange(1, H + 1, dtype=F32) / H)
    in_specs = [pl.BlockSpec(memory_space=pltpu.SMEM)]
    args = [slopes]
    for a, dil in zip(qkvs, dils):
        L = S // dil
        in_specs += [
            pl.BlockSpec((1, 1, dil, tt // dil, dh), lambda b, h, t: (b, h, 0, t, 0)),
            pl.BlockSpec((1, 1, dil, L, dh), lambda b, h, t: (b, H + h, 0, 0, 0)),
            pl.BlockSpec((1, 1, dil, L, dh), lambda b, h, t: (b, 2 * H + h, 0, 0, 0))]
        args += [a, a, a]
    return pl.pallas_call(
        functools.partial(_dil_body, tt=tt, dils=dils),
        grid=(B, H, S // tt),
        in_specs=in_specs,
        out_specs=pl.BlockSpec((1, 1, tt, dh), lambda b, h, t: (b, h, t, 0)),
        out_shape=jax.ShapeDtypeStruct((B, H, S, dh), BF16),
        scratch_shapes=[pltpu.VMEM((tt, LANES), F32)] * (3 * len(dils)),
        compiler_params=_params("parallel", "parallel", "arbitrary"),
        name="dilated_attn",
    )(*args)


def _gate_body(x_ref, wg_ref, w2_ref, b_ref, o_ref):
    C = GLA_CHUNK
    g_low = jnp.dot(x_ref[0].astype(BF16), wg_ref[...], preferred_element_type=F32)
    z = jnp.dot(g_low.astype(BF16), w2_ref[...], preferred_element_type=F32) + b_ref[...]
    log_a = (jnp.minimum(z, 0.0) - jnp.log1p(jnp.exp(-jnp.abs(z)))) / GLA_TAU
    tri = (lax.broadcasted_iota(jnp.int32, (C, C), 0)
           >= lax.broadcasted_iota(jnp.int32, (C, C), 1)).astype(BF16)
    for c in range(log_a.shape[0] // C):
        a = log_a[c * C:(c + 1) * C]
        hi = a.astype(BF16)
        rem = a - hi.astype(F32)
        mid = rem.astype(BF16)
        lo = (rem - mid.astype(F32)).astype(BF16)
        cum = (jnp.dot(tri, hi, preferred_element_type=F32)
               + jnp.dot(tri, mid, preferred_element_type=F32)
               + jnp.dot(tri, lo, preferred_element_type=F32))
        o_ref[0, c * C:(c + 1) * C, :] = cum


def _gla_gate(x, wg, w2, b, *, tm=512):
    B, S, D = x.shape
    tm = min(tm, S)
    return pl.pallas_call(
        _gate_body,
        grid=(B, S // tm),
        in_specs=[pl.BlockSpec((1, tm, D), lambda b_, i: (b_, i, 0)),
                  pl.BlockSpec((D, LANES), lambda b_, i: (0, 0)),
                  pl.BlockSpec((LANES, GLA_DK), lambda b_, i: (0, 0)),
                  pl.BlockSpec((1, GLA_DK), lambda b_, i: (0, 0))],
        out_specs=pl.BlockSpec((1, tm, GLA_DK), lambda b_, i: (b_, i, 0)),
        out_shape=jax.ShapeDtypeStruct((B, S, GLA_DK), F32),
        compiler_params=_params("parallel", "parallel"),
        name="gla_gate",
    )(x, wg, w2, b.reshape(1, GLA_DK))


def _gla_body(q_ref, k_ref, v_ref, r_ref, b_ref, g_ref, o_ref, st_ref):
    C = GLA_CHUNK
    dk = GLA_DK // GLA_HEADS

    @pl.when(pl.program_id(2) == 0)
    def _():
        st_ref[...] = jnp.zeros_like(st_ref)

    causal = (lax.broadcasted_iota(jnp.int32, (C, C), 0)
              >= lax.broadcasted_iota(jnp.int32, (C, C), 1))
    st = st_ref[...]
    for c in range(q_ref.shape[1] // C):
        sl = slice(c * C, (c + 1) * C)
        bc = b_ref[0, sl, :]
        bl = bc[C - 1:C, :]
        kk = k_ref[0, sl, :]
        q_dec = (q_ref[0, sl, :] * (dk ** -0.5) * jnp.exp(bc)).astype(BF16)
        k_intra = (kk * jnp.exp(-bc)).astype(BF16)
        k_state = (kk * jnp.exp(bl - bc)).astype(BF16)
        vb = v_ref[0, sl, :].astype(BF16)
        s = jnp.where(causal, _dot_nt(q_dec, k_intra), 0.0)
        o = (jnp.dot(s.astype(BF16), vb, preferred_element_type=F32)
             + _dot_nt(q_dec, st.astype(BF16)))
        st = st * jnp.exp(bl) + lax.dot_general(
            vb, k_state, (((0,), (0,)), ((), ())), preferred_element_type=F32)
        o = o * lax.rsqrt(jnp.mean(o * o, axis=-1, keepdims=True) + RMS_EPS) * g_ref[...]
        r = r_ref[0, sl, :]
        o_ref[0, 0, sl, :] = (o * (r * jax.nn.sigmoid(r))).astype(o_ref.dtype)
    st_ref[...] = st


def _gla(proj, bcum, gnorm_g, *, ts=512):
    B, S, _ = proj.shape
    H = GLA_HEADS
    dk, dv = GLA_DK // H, GLA_DV // H
    ts = min(ts, S)
    nq = GLA_DK // dk
    nv = 2 * GLA_DK // dv
    return pl.pallas_call(
        _gla_body,
        grid=(B, H, S // ts),
        in_specs=[pl.BlockSpec((1, ts, dk), lambda b, h, t: (b, t, h)),
                  pl.BlockSpec((1, ts, dk), lambda b, h, t: (b, t, nq + h)),
                  pl.BlockSpec((1, ts, dv), lambda b, h, t: (b, t, nv + h)),
                  pl.BlockSpec((1, ts, dv), lambda b, h, t: (b, t, nv + H + h)),
                  pl.BlockSpec((1, ts, dk), lambda b, h, t: (b, t, h)),
                  pl.BlockSpec((1, dv), lambda b, h, t: (0, 0))],
        out_specs=pl.BlockSpec((1, 1, ts, dv), lambda b, h, t: (b, h, t, 0)),
        out_shape=jax.ShapeDtypeStruct((B, H, S, dv), BF16),
        scratch_shapes=[pltpu.VMEM((dv, dk), F32)],
        compiler_params=_params("parallel", "parallel", "arbitrary"),
        name="gla",
    )(proj, proj, proj, proj, bcum, gnorm_g.reshape(1, dv))


def _diff_layer(x, w_in, lq1, lk1, lq2, lk2, subln_g, w_out, g1, b1, layer_idx):
    lambda_init = 0.8 - 0.6 * math.exp(-0.3 * layer_idx)
    qkv = _proj(x, w_in.astype(BF16))
    o = _diff_attention(qkv, lq1, lk1, lq2, lk2, subln_g, lambda_init)
    return _out_ln(o, w_out.astype(BF16), x, g1, b1)


def _dil_layer(x, w_in, w_out, g1, b1):
    wb = w_in.astype(BF16)
    per_group = 3 * DIL_HEADS * DIL_HEAD_DIM
    qkvs = [_proj(x, wb[:, g * per_group:(g + 1) * per_group], dil=dil)
            for g, (_, dil) in enumerate(DIL_GROUPS)]
    o = _dilated_attention(qkvs)
    return _out_ln(o, w_out.astype(BF16), x, g1, b1)


def _gla_layer(x, w_in, w_gate2, b_gate, gnorm_g, w_out, g1, b1):
    n_main = 2 * GLA_DK + 2 * GLA_DV
    wb = w_in.astype(BF16)
    proj = _proj(x, wb[:, :n_main], head_major=False, out_dtype=F32)
    wg = jnp.pad(wb[:, n_main:], ((0, 0), (0, LANES - GLA_GATE_RANK)))
    w2 = jnp.pad(w_gate2.astype(BF16), ((0, LANES - GLA_GATE_RANK), (0, 0)))
    bcum = _gla_gate(x, wg, w2, b_gate)
    o = _gla(proj, bcum, gnorm_g)
    return _out_ln(o, w_out.astype(BF16), x, g1, b1)


def kernel(x, l0_w_in, l0_lam_q1, l0_lam_k1, l0_lam_q2, l0_lam_k2, l0_subln_g, l0_w_out, l0_ln1_g, l0_ln1_b, l0_w_ff1, l0_w_ff2, l0_ln2_g, l0_ln2_b, l1_w_in, l1_w_out, l1_ln1_g, l1_ln1_b, l1_w_ff1, l1_w_ff2, l1_ln2_g, l1_ln2_b, l2_w_in, l2_w_gate2, l2_b_gate, l2_gnorm_g, l2_w_out, l2_ln1_g, l2_ln1_b, l2_w_ff1, l2_w_ff2, l2_ln2_g, l2_ln2_b, l3_w_in, l3_lam_q1, l3_lam_k1, l3_lam_q2, l3_lam_k2, l3_subln_g, l3_w_out, l3_ln1_g, l3_ln1_b, l3_w_ff1, l3_w_ff2, l3_ln2_g, l3_ln2_b):
    x = _diff_layer(x, l0_w_in, l0_lam_q1, l0_lam_k1, l0_lam_q2, l0_lam_k2, l0_subln_g,
                    l0_w_out, l0_ln1_g, l0_ln1_b, 0)
    x = _mlp_ln(x, l0_w_ff1.astype(BF16), l0_w_ff2.astype(BF16), l0_ln2_g, l0_ln2_b)
    x = _dil_layer(x, l1_w_in, l1_w_out, l1_ln1_g, l1_ln1_b)
    x = _mlp_ln(x, l1_w_ff1.astype(BF16), l1_w_ff2.astype(BF16), l1_ln2_g, l1_ln2_b)
    x = _gla_layer(x, l2_w_in, l2_w_gate2, l2_b_gate, l2_gnorm_g, l2_w_out, l2_ln1_g, l2_ln1_b)
    x = _mlp_ln(x, l2_w_ff1.astype(BF16), l2_w_ff2.astype(BF16), l2_ln2_g, l2_ln2_b)
    x = _diff_layer(x, l3_w_in, l3_lam_q1, l3_lam_k1, l3_lam_q2, l3_lam_k2, l3_subln_g,
                    l3_w_out, l3_ln1_g, l3_ln1_b, 3)
    x = _mlp_ln(x, l3_w_ff1.astype(BF16), l3_w_ff2.astype(BF16), l3_ln2_g, l3_ln2_b)
    return x
```

```python
import functools
import math

import jax
import jax.numpy as jnp
from jax import lax
from jax.experimental import pallas as pl
from jax.experimental.pallas import tpu as pltpu

F32 = jnp.float32
BF16 = jnp.bfloat16

D_MODEL = 1024
DEPTH = 4
N_MIXERS = 3
DA_HEADS = 8
DA_HEAD_DIM = 64
DIL_GROUPS = ((128, 1), (512, 4), (2048, 16))
DIL_HEADS = 8
DIL_HEAD_DIM = 128
DIL_BLOCK = 128
DIL_UNROLL = 8
GLA_HEADS = 4
GLA_DK = D_MODEL // 2
GLA_DV = D_MODEL
GLA_GATE_RANK = 16
GLA_TAU = 16.0
GLA_CHUNK = 64
D_FF = 4 * D_MODEL
ALPHA = (2 * DEPTH) ** 0.25
LN_EPS = 1e-5
RMS_EPS = 1e-6

LANES = 128
VMEM_LIMIT = 56 * 1024 * 1024
NEG = -1e30
LOG2E = math.log2(math.e)
SUM_ROWS = 16


def _params(*sem):
    return pltpu.CompilerParams(dimension_semantics=sem, vmem_limit_bytes=VMEM_LIMIT)


def _layer_norm(z, g, b):
    mu = jnp.mean(z, axis=-1, keepdims=True)
    zc = z - mu
    var = jnp.mean(zc * zc, axis=-1, keepdims=True)
    return zc * lax.rsqrt(var + LN_EPS) * g + b


def _dot_nt(a, b):
    return lax.dot_general(a, b, (((1,), (1,)), ((), ())), preferred_element_type=F32)


def _proj_body(x_ref, w_ref, o_ref, xb_ref, *acc_scratch, dil, head_major):
    @pl.when(pl.program_id(2) == 0)
    def _():
        xb_ref[...] = x_ref[0].astype(BF16)

    acc = jnp.dot(xb_ref[...], w_ref[...], preferred_element_type=F32)
    tm, tn = acc.shape
    if not head_major:
        o_ref[0] = acc.astype(o_ref.dtype)
    elif dil == 1:
        for c in range(tn // LANES):
            o_ref[0, c, 0] = acc[:, c * LANES:(c + 1) * LANES].astype(o_ref.dtype)
    else:
        acc_ref, = acc_scratch
        rows = tm // dil
        for c in range(tn // LANES):
            acc_ref[c] = acc[:, c * LANES:(c + 1) * LANES]
            for r in range(dil):
                o_ref[0, c, r] = acc_ref[c, pl.ds(r, rows, stride=dil), :].astype(o_ref.dtype)


def _proj(x, w, *, dil=1, head_major=True, out_dtype=BF16, tm=1024, tn=1024):
    B, S, D = x.shape
    N = w.shape[1]
    tm = min(tm, S)
    tn = min(tn, N)
    assert S % tm == 0 and N % tn == 0 and tm % dil == 0 and tn % LANES == 0
    if head_major:
        out_shape = jax.ShapeDtypeStruct((B, N // LANES, dil, S // dil, LANES), out_dtype)
        out_spec = pl.BlockSpec((1, tn // LANES, dil, tm // dil, LANES),
                                lambda b, i, j: (b, j, 0, i, 0))
    else:
        out_shape = jax.ShapeDtypeStruct((B, S, N), out_dtype)
        out_spec = pl.BlockSpec((1, tm, tn), lambda b, i, j: (b, i, j))
    scratch = [pltpu.VMEM((tm, D), BF16)]
    if head_major and dil > 1:
        scratch.append(pltpu.VMEM((tn // LANES, tm, LANES), F32))
    return pl.pallas_call(
        functools.partial(_proj_body, dil=dil, head_major=head_major),
        grid=(B, S // tm, N // tn),
        in_specs=[pl.BlockSpec((1, tm, D), lambda b, i, j: (b, i, 0)),
                  pl.BlockSpec((D, tn), lambda b, i, j: (0, j))],
        out_specs=out_spec,
        out_shape=out_shape,
        scratch_shapes=scratch,
        compiler_params=_params("parallel", "parallel", "arbitrary"),
        name="proj_in",
    )(x, w)


def _out_ln_body(o_ref, w_ref, x_ref, g_ref, b_ref, y_ref, *, nblk):
    o = jnp.concatenate([o_ref[0, c] for c in range(nblk)], axis=-1)
    y = jnp.dot(o, w_ref[...], preferred_element_type=F32)
    y_ref[0] = _layer_norm(ALPHA * x_ref[0] + y, g_ref[...], b_ref[...])


def _out_ln(o, w, x, g, b, *, tm=512):
    B, nblk, S, wblk = o.shape
    D = x.shape[-1]
    tm = min(tm, S)
    return pl.pallas_call(
        functools.partial(_out_ln_body, nblk=nblk),
        grid=(B, S // tm),
        in_specs=[pl.BlockSpec((1, nblk, tm, wblk), lambda b, i: (b, 0, i, 0)),
                  pl.BlockSpec((nblk * wblk, D), lambda b, i: (0, 0)),
                  pl.BlockSpec((1, tm, D), lambda b, i: (b, i, 0)),
                  pl.BlockSpec((1, D), lambda b, i: (0, 0)),
                  pl.BlockSpec((1, D), lambda b, i: (0, 0))],
        out_specs=pl.BlockSpec((1, tm, D), lambda b, i: (b, i, 0)),
        out_shape=jax.ShapeDtypeStruct(x.shape, F32),
        compiler_params=_params("parallel", "parallel"),
        name="out_proj_ln",
    )(o, w, x, g.reshape(1, D), b.reshape(1, D))


def _mlp_body(x_ref, w1_ref, w2_ref, g_ref, b_ref, y_ref, xb_ref, acc_ref):
    k = pl.program_id(2)

    @pl.when(k == 0)
    def _():
        xb_ref[...] = x_ref[0].astype(BF16)
        acc_ref[...] = jnp.zeros_like(acc_ref)

    h = jnp.dot(xb_ref[...], w1_ref[...], preferred_element_type=F32)
    h = jnp.square(jnp.maximum(h, 0.0)).astype(BF16)
    acc_ref[...] += jnp.dot(h, w2_ref[...], preferred_element_type=F32)

    @pl.when(k == pl.num_programs(2) - 1)
    def _():
        y_ref[0] = _layer_norm(ALPHA * x_ref[0] + acc_ref[...], g_ref[...], b_ref[...])


def _mlp_ln(x, w1, w2, g, b, *, tm=1024, tf=512):
    B, S, D = x.shape
    FF = w1.shape[1]
    tm = min(tm, S)
    return pl.pallas_call(
        _mlp_body,
        grid=(B, S // tm, FF // tf),
        in_specs=[pl.BlockSpec((1, tm, D), lambda b, i, k: (b, i, 0)),
                  pl.BlockSpec((D, tf), lambda b, i, k: (0, k)),
                  pl.BlockSpec((tf, D), lambda b, i, k: (k, 0)),
                  pl.BlockSpec((1, D), lambda b, i, k: (0, 0)),
                  pl.BlockSpec((1, D), lambda b, i, k: (0, 0))],
        out_specs=pl.BlockSpec((1, tm, D), lambda b, i, k: (b, i, 0)),
        out_shape=jax.ShapeDtypeStruct(x.shape, F32),
        scratch_shapes=[pltpu.VMEM((tm, D), BF16), pltpu.VMEM((tm, D), F32)],
        compiler_params=_params("parallel", "parallel", "arbitrary"),
        name="mlp_ln",
    )(x, w1, w2, g.reshape(1, D), b.reshape(1, D))


def _diff_body(slopes_ref, q_ref, k_ref, v_ref, lq1_ref, lk1_ref, lq2_ref, lk2_ref, g_ref,
               o_ref, vt_ref, bias_ref, qm_ref, sa_ref, sb_ref, pa_ref, pb_ref, acc_ref,
               *, tile, lambda_init):
    h = pl.program_id(1)
    i = pl.program_id(2)
    d = DA_HEAD_DIM
    S = k_ref.shape[3]
    slope = slopes_ref[h] * LOG2E

    @pl.when(i == 0)
    def _():
        vch = min(512, S)
        for c in range(S // vch):
            sl = slice(c * vch, (c + 1) * vch)
            vt_ref[:2 * d, sl] = v_ref[0, 0, 0, sl, :].astype(F32).T.astype(BF16)
        vt_ref[2 * d:, :] = jnp.ones((SUM_ROWS, S), BF16)
        rel = (lax.broadcasted_iota(jnp.int32, (tile, tile), 0)
               - lax.broadcasted_iota(jnp.int32, (tile, tile), 1))
        bias = slope * rel.astype(F32)
        bias_ref[0] = bias
        bias_ref[1] = jnp.where(rel <= 0, bias, NEG)
        bias_ref[2] = jnp.full((tile, tile), NEG, F32)

    q = q_ref[0, 0, 0].astype(F32) * (d ** -0.5 * LOG2E)
    lane = lax.broadcasted_iota(jnp.int32, q.shape, 1)
    qm_ref[0] = jnp.where(lane < d, q, 0.0).astype(BF16)
    qm_ref[1] = jnp.where(lane >= d, q, 0.0).astype(BF16)
    acc_ref[...] = jnp.zeros_like(acc_ref)
    pb_ref[...] = jnp.zeros_like(pb_ref)

    def scores(t, s_ref):
        kind = (t >= i).astype(jnp.int32) + (t > i).astype(jnp.int32)
        k0 = pl.multiple_of(jnp.minimum(t, i) * tile, tile)
        kt = k_ref[0, 0, 0, pl.ds(k0, tile), :]
        bias = bias_ref[kind]
        cms = []
        for mp in range(2):
            s = _dot_nt(kt, qm_ref[mp]) + bias
            s_ref[mp] = s
            cms.append(jnp.max(s, axis=0, keepdims=True))
        return tuple(cms)

    def values(t, p_ref, a):
        k0 = pl.multiple_of(jnp.clip(t, 0, i) * tile, tile)
        vt = vt_ref[:, pl.ds(k0, tile)]
        for mp in range(2):
            acc_ref[mp] = a[mp] * acc_ref[mp] + jnp.dot(vt, p_ref[mp],
                                                        preferred_element_type=F32)

    def step(t, s_cur, s_nxt, p_cur, p_prev, carry):
        m, a, cm = carry
        cm_nxt = scores(t + 1, s_nxt)
        values(t - 1, p_prev, a)
        off = slope * ((t - i) * tile).astype(F32)
        m_new, a_new = [], []
        for mp in range(2):
            mn = jnp.maximum(m[mp], cm[mp] + off)
            an = jnp.exp2(m[mp] - mn)
            p_cur[mp] = jnp.exp2(s_cur[mp] - (mn - off)).astype(BF16)
            m_new.append(mn)
            a_new.append(an)
        return tuple(m_new), tuple(a_new), cm_nxt

    def pair(pp, carry):
        carry = step(2 * pp, sa_ref, sb_ref, pa_ref, pb_ref, carry)
        return step(2 * pp + 1, sb_ref, sa_ref, pb_ref, pa_ref, carry)

    two = lambda x: (x, x)
    carry = (two(jnp.full((1, tile), NEG, F32)), two(jnp.ones((1, tile), F32)),
             scores(0, sa_ref))
    n_pairs = (i + 2) // 2
    _, a, _ = lax.fori_loop(0, n_pairs, pair, carry)
    values(2 * n_pairs - 1, pb_ref, a)

    lam = (jnp.exp(jnp.sum(lq1_ref[...] * lk1_ref[...], axis=-1, keepdims=True))
           - jnp.exp(jnp.sum(lq2_ref[...] * lk2_ref[...], axis=-1, keepdims=True))
           + lambda_init)
    o1 = acc_ref[0, :2 * d] / acc_ref[0, 2 * d:2 * d + 1]
    o2 = acc_ref[1, :2 * d] / acc_ref[1, 2 * d:2 * d + 1]
    o = o1 - lam * o2
    o = o * lax.rsqrt(jnp.mean(o * o, axis=0, keepdims=True) + RMS_EPS)
    o_ref[0, 0] = (o.T * g_ref[...] * (1.0 - lambda_init)).astype(o_ref.dtype)


def _diff_attention(qkv, lq1, lk1, lq2, lk2, subln_g, lambda_init, *, tile=256):
    B, _, _, S, _ = qkv.shape
    H, d = DA_HEADS, DA_HEAD_DIM
    tq = tile = min(tile, S)
    assert S % tile == 0
    slopes = 2.0 ** (-8.0 * jnp.arange(1, H + 1, dtype=F32) / H)
    vec = lambda a: a.reshape(1, -1).astype(F32)
    small = lambda n: pl.BlockSpec((1, n), lambda b, h, i: (0, 0))
    scores = pltpu.VMEM((2, tile, tile), F32)
    probs = pltpu.VMEM((2, tile, tile), BF16)
    return pl.pallas_call(
        functools.partial(_diff_body, tile=tile, lambda_init=lambda_init),
        grid=(B, H, S // tq),
        in_specs=[pl.BlockSpec(memory_space=pltpu.SMEM),
                  pl.BlockSpec((1, 1, 1, tq, 2 * d), lambda b, h, i: (b, h, 0, i, 0)),
                  pl.BlockSpec((1, 1, 1, S, 2 * d), lambda b, h, i: (b, H + h, 0, 0, 0)),
                  pl.BlockSpec((1, 1, 1, S, 2 * d), lambda b, h, i: (b, 2 * H + h, 0, 0, 0)),
                  small(d), small(d), small(d), small(d), small(2 * d)],
        out_specs=pl.BlockSpec((1, 1, tq, 2 * d), lambda b, h, i: (b, h, i, 0)),
        out_shape=jax.ShapeDtypeStruct((B, H, S, 2 * d), BF16),
        scratch_shapes=[pltpu.VMEM((2 * d + SUM_ROWS, S), BF16),
                        pltpu.VMEM((3, tile, tile), F32),
                        pltpu.VMEM((2, tq, 2 * d), BF16),
                        scores, scores, probs, probs,
                        pltpu.VMEM((2, 2 * d + SUM_ROWS, tq), F32)],
        compiler_params=_params("parallel", "parallel", "arbitrary"),
        name="diff_attn",
    )(slopes, qkv, qkv, qkv, vec(lq1), vec(lk1), vec(lq2), vec(lk2), vec(subln_g))


def _dil_body(slopes_ref, *refs, tt, dils):
    G = len(dils)
    in_refs, o_ref = refs[:3 * G], refs[3 * G]
    acc_refs = refs[3 * G + 1:3 * G + 1 + G]
    m_refs = refs[3 * G + 1 + G:3 * G + 1 + 2 * G]
    l_refs = refs[3 * G + 1 + 2 * G:3 * G + 1 + 3 * G]
    h = pl.program_id(1)
    t = pl.program_id(2)
    slope = slopes_ref[h]
    blk = DIL_BLOCK
    scale = DIL_HEAD_DIM ** -0.5
    base = (lax.broadcasted_iota(jnp.int32, (blk, 2 * blk), 0)
            - lax.broadcasted_iota(jnp.int32, (blk, 2 * blk), 1))

    for g in range(G):
        dil = dils[g]
        window = DIL_GROUPS[g][0] // dil
        q_ref, k_ref, v_ref = in_refs[3 * g:3 * g + 3]
        rows = tt // dil
        nb = rows // blk

        def block(idx, _, dil=dil, window=window, q_ref=q_ref, k_ref=k_ref, v_ref=v_ref,
                  rows=rows, nb=nb, g=g):
            r = idx // nb
            bi = idx % nb
            lq = pl.multiple_of(bi * blk, blk)
            l0 = t * rows + lq
            ks = pl.multiple_of(jnp.maximum(l0 - blk, 0), blk)
            qb = q_ref[0, 0, r, pl.ds(lq, blk), :]
            kc = k_ref[0, 0, r, pl.ds(ks, 2 * blk), :]
            vc = v_ref[0, 0, r, pl.ds(ks, 2 * blk), :]
            dist = base + (l0 - ks)
            s = _dot_nt(qb, kc) * scale - (slope * dil) * dist.astype(F32)
            s = jnp.where((dist >= 0) & (dist <= window), s, NEG)
            m = jnp.max(s, axis=-1, keepdims=True)
            p = jnp.exp(s - m)
            l = jnp.sum(p, axis=-1, keepdims=True)
            acc = jnp.dot(p.astype(BF16), vc, preferred_element_type=F32)
            tok = pl.ds(lq * dil + r, blk, stride=dil)
            acc_refs[g][tok, :] = acc
            m_refs[g][tok, :] = jnp.broadcast_to(m, (blk, LANES))
            l_refs[g][tok, :] = jnp.broadcast_to(l, (blk, LANES))
            return 0

        lax.fori_loop(0, dil * nb, block, 0, unroll=DIL_UNROLL)

    ch = 256
    for c in range(tt // ch):
        sl = slice(c * ch, (c + 1) * ch)
        ms = [m_refs[g][sl, :] for g in range(G)]
        top = functools.reduce(jnp.maximum, ms)
        ws = [jnp.exp(m - top) for m in ms]
        num = sum(w * acc_refs[g][sl, :] for g, w in enumerate(ws))
        den = sum(w * l_refs[g][sl, :] for g, w in enumerate(ws))
        o_ref[0, 0, sl, :] = (num / den).astype(o_ref.dtype)


def _dilated_attention(qkvs, *, tt=2048):
    dils = tuple(d for _, d in DIL_GROUPS)
    B = qkvs[0].shape[0]
    S = qkvs[0].shape[2] * qkvs[0].shape[3]
    H, dh = DIL_HEADS, DIL_HEAD_DIM
    tt = min(tt, S)
    assert all(tt % (d * DIL_BLOCK) == 0 and S // d >= 2 * DIL_BLOCK for d in dils)
    slopes = 2.0 ** (-8.0 * jnp.arange(1, H + 1, dtype=F32) / H)
    in_specs = [pl.BlockSpec(memory_space=pltpu.SMEM)]
    args = [slopes]
    for a, dil in zip(qkvs, dils):
        L = S // dil
        in_specs += [
            pl.BlockSpec((1, 1, dil, tt // dil, dh), lambda b, h, t: (b, h, 0, t, 0)),
            pl.BlockSpec((1, 1, dil, L, dh), lambda b, h, t: (b, H + h, 0, 0, 0)),
            pl.BlockSpec((1, 1, dil, L, dh), lambda b, h, t: (b, 2 * H + h, 0, 0, 0))]
        args += [a, a, a]
    return pl.pallas_call(
        functools.partial(_dil_body, tt=tt, dils=dils),
        grid=(B, H, S // tt),
        in_specs=in_specs,
        out_specs=pl.BlockSpec((1, 1, tt, dh), lambda b, h, t: (b, h, t, 0)),
        out_shape=jax.ShapeDtypeStruct((B, H, S, dh), BF16),
        scratch_shapes=[pltpu.VMEM((tt, LANES), F32)] * (3 * len(dils)),
        compiler_params=_params("parallel", "parallel", "arbitrary"),
        name="dilated_attn",
    )(*args)


def _gate_body(x_ref, wg_ref, w2_ref, b_ref, o_ref):
    C = GLA_CHUNK
    g_low = jnp.dot(x_ref[0].astype(BF16), wg_ref[...], preferred_element_type=F32)
    z = jnp.dot(g_low.astype(BF16), w2_ref[...], preferred_element_type=F32) + b_ref[...]
    log_a = (jnp.minimum(z, 0.0) - jnp.log1p(jnp.exp(-jnp.abs(z)))) / GLA_TAU
    tri = (lax.broadcasted_iota(jnp.int32, (C, C), 0)
           >= lax.broadcasted_iota(jnp.int32, (C, C), 1)).astype(BF16)
    for c in range(log_a.shape[0] // C):
        a = log_a[c * C:(c + 1) * C]
        hi = a.astype(BF16)
        rem = a - hi.astype(F32)
        mid = rem.astype(BF16)
        lo = (rem - mid.astype(F32)).astype(BF16)
        cum = (jnp.dot(tri, hi, preferred_element_type=F32)
               + jnp.dot(tri, mid, preferred_element_type=F32)
               + jnp.dot(tri, lo, preferred_element_type=F32))
        o_ref[0, c * C:(c + 1) * C, :] = cum


def _gla_gate(x, wg, w2, b, *, tm=512):
    B, S, D = x.shape
    tm = min(tm, S)
    return pl.pallas_call(
        _gate_body,
        grid=(B, S // tm),
        in_specs=[pl.BlockSpec((1, tm, D), lambda b_, i: (b_, i, 0)),
                  pl.BlockSpec((D, LANES), lambda b_, i: (0, 0)),
                  pl.BlockSpec((LANES, GLA_DK), lambda b_, i: (0, 0)),
                  pl.BlockSpec((1, GLA_DK), lambda b_, i: (0, 0))],
        out_specs=pl.BlockSpec((1, tm, GLA_DK), lambda b_, i: (b_, i, 0)),
        out_shape=jax.ShapeDtypeStruct((B, S, GLA_DK), F32),
        compiler_params=_params("parallel", "parallel"),
        name="gla_gate",
    )(x, wg, w2, b.reshape(1, GLA_DK))


def _gla_body(q_ref, k_ref, v_ref, r_ref, b_ref, g_ref, o_ref, st_ref):
    C = GLA_CHUNK
    dk = GLA_DK // GLA_HEADS

    @pl.when(pl.program_id(2) == 0)
    def _():
        st_ref[...] = jnp.zeros_like(st_ref)

    causal = (lax.broadcasted_iota(jnp.int32, (C, C), 0)
              >= lax.broadcasted_iota(jnp.int32, (C, C), 1))
    st = st_ref[...]
    for c in range(q_ref.shape[1] // C):
        sl = slice(c * C, (c + 1) * C)
        bc = b_ref[0, sl, :]
        bl = bc[C - 1:C, :]
        kk = k_ref[0, sl, :]
        q_dec = (q_ref[0, sl, :] * (dk ** -0.5) * jnp.exp(bc)).astype(BF16)
        k_intra = (kk * jnp.exp(-bc)).astype(BF16)
        k_state = (kk * jnp.exp(bl - bc)).astype(BF16)
        vb = v_ref[0, sl, :].astype(BF16)
        s = jnp.where(causal, _dot_nt(q_dec, k_intra), 0.0)
        o = (jnp.dot(s.astype(BF16), vb, preferred_element_type=F32)
             + _dot_nt(q_dec, st.astype(BF16)))
        st = st * jnp.exp(bl) + lax.dot_general(
            vb, k_state, (((0,), (0,)), ((), ())), preferred_element_type=F32)
        o = o * lax.rsqrt(jnp.mean(o * o, axis=-1, keepdims=True) + RMS_EPS) * g_ref[...]
        r = r_ref[0, sl, :]
        o_ref[0, 0, sl, :] = (o * (r * jax.nn.sigmoid(r))).astype(o_ref.dtype)
    st_ref[...] = st


def _gla(proj, bcum, gnorm_g, *, ts=512):
    B, S, _ = proj.shape
    H = GLA_HEADS
    dk, dv = GLA_DK // H, GLA_DV // H
    ts = min(ts, S)
    nq = GLA_DK // dk
    nv = 2 * GLA_DK // dv
    return pl.pallas_call(
        _gla_body,
        grid=(B, H, S // ts),
        in_specs=[pl.BlockSpec((1, ts, dk), lambda b, h, t: (b, t, h)),
                  pl.BlockSpec((1, ts, dk), lambda b, h, t: (b, t, nq + h)),
                  pl.BlockSpec((1, ts, dv), lambda b, h, t: (b, t, nv + h)),
                  pl.BlockSpec((1, ts, dv), lambda b, h, t: (b, t, nv + H + h)),
                  pl.BlockSpec((1, ts, dk), lambda b, h, t: (b, t, h)),
                  pl.BlockSpec((1, dv), lambda b, h, t: (0, 0))],
        out_specs=pl.BlockSpec((1, 1, ts, dv), lambda b, h, t: (b, h, t, 0)),
        out_shape=jax.ShapeDtypeStruct((B, H, S, dv), BF16),
        scratch_shapes=[pltpu.VMEM((dv, dk), F32)],
        compiler_params=_params("parallel", "parallel", "arbitrary"),
        name="gla",
    )(proj, proj, proj, proj, bcum, gnorm_g.reshape(1, dv))


def _diff_layer(x, w_in, lq1, lk1, lq2, lk2, subln_g, w_out, g1, b1, layer_idx):
    lambda_init = 0.8 - 0.6 * math.exp(-0.3 * layer_idx)
    qkv = _proj(x, w_in.astype(BF16))
    o = _diff_attention(qkv, lq1, lk1, lq2, lk2, subln_g, lambda_init)
    return _out_ln(o, w_out.astype(BF16), x, g1, b1)


def _dil_layer(x, w_in, w_out, g1, b1):
    wb = w_in.astype(BF16)
    per_group = 3 * DIL_HEADS * DIL_HEAD_DIM
    qkvs = [_proj(x, wb[:, g * per_group:(g + 1) * per_group], dil=dil)
            for g, (_, dil) in enumerate(DIL_GROUPS)]
    o = _dilated_attention(qkvs)
    return _out_ln(o, w_out.astype(BF16), x, g1, b1)


def _gla_layer(x, w_in, w_gate2, b_gate, gnorm_g, w_out, g1, b1):
    n_main = 2 * GLA_DK + 2 * GLA_DV
    wb = w_in.astype(BF16)
    proj = _proj(x, wb[:, :n_main], head_major=False, out_dtype=F32)
    wg = jnp.pad(wb[:, n_main:], ((0, 0), (0, LANES - GLA_GATE_RANK)))
    w2 = jnp.pad(w_gate2.astype(BF16), ((0, LANES - GLA_GATE_RANK), (0, 0)))
    bcum = _gla_gate(x, wg, w2, b_gate)
    o = _gla(proj, bcum, gnorm_g)
    return _out_ln(o, w_out.astype(BF16), x, g1, b1)


def kernel(x, l0_w_in, l0_lam_q1, l0_lam_k1, l0_lam_q2, l0_lam_k2, l0_subln_g, l0_w_out, l0_ln1_g, l0_ln1_b, l0_w_ff1, l0_w_ff2, l0_ln2_g, l0_ln2_b, l1_w_in, l1_w_out, l1_ln1_g, l1_ln1_b, l1_w_ff1, l1_w_ff2, l1_ln2_g, l1_ln2_b, l2_w_in, l2_w_gate2, l2_b_gate, l2_gnorm_g, l2_w_out, l2_ln1_g, l2_ln1_b, l2_w_ff1, l2_w_ff2, l2_ln2_g, l2_ln2_b, l3_w_in, l3_lam_q1, l3_lam_k1, l3_lam_q2, l3_lam_k2, l3_subln_g, l3_w_out, l3_ln1_g, l3_ln1_b, l3_w_ff1, l3_w_ff2, l3_ln2_g, l3_ln2_b):
    x = _diff_layer(x, l0_w_in, l0_lam_q1, l0_lam_k1, l0_lam_q2, l0_lam_k2, l0_subln_g,
                    l0_w_out, l0_ln1_g, l0_ln1_b, 0)
    x = _mlp_ln(x, l0_w_ff1.astype(BF16), l0_w_ff2.astype(BF16), l0_ln2_g, l0_ln2_b)
    x = _dil_layer(x, l1_w_in, l1_w_out, l1_ln1_g, l1_ln1_b)
    x = _mlp_ln(x, l1_w_ff1.astype(BF16), l1_w_ff2.astype(BF16), l1_ln2_g, l1_ln2_b)
    x = _gla_layer(x, l2_w_in, l2_w_gate2, l2_b_gate, l2_gnorm_g, l2_w_out, l2_ln1_g, l2_ln1_b)
    x = _mlp_ln(x, l2_w_ff1.astype(BF16), l2_w_ff2.astype(BF16), l2_ln2_g, l2_ln2_b)
    x = _diff_layer(x, l3_w_in, l3_lam_q1, l3_lam_k1, l3_lam_q2, l3_lam_k2, l3_subln_g,
                    l3_w_out, l3_ln1_g, l3_ln1_b, 3)
    x = _mlp_ln(x, l3_w_ff1.astype(BF16), l3_w_ff2.astype(BF16), l3_ln2_g, l3_ln2_b)
    return x
```

```python
import functools
import math

import jax
import jax.numpy as jnp
from jax import lax
from jax.experimental import pallas as pl
from jax.experimental.pallas import tpu as pltpu

F32 = jnp.float32
BF16 = jnp.bfloat16

D_MODEL = 1024
DEPTH = 4
N_MIXERS = 3
DA_HEADS = 8
DA_HEAD_DIM = 64
DIL_GROUPS = ((128, 1), (512, 4), (2048, 16))
DIL_HEADS = 8
DIL_HEAD_DIM = 128
DIL_BLOCK = 128
DIL_UNROLL = 16
GLA_HEADS = 4
GLA_DK = D_MODEL // 2
GLA_DV = D_MODEL
GLA_GATE_RANK = 16
GLA_TAU = 16.0
GLA_CHUNK = 64
D_FF = 4 * D_MODEL
ALPHA = (2 * DEPTH) ** 0.25
LN_EPS = 1e-5
RMS_EPS = 1e-6

LANES = 128
VMEM_LIMIT = 56 * 1024 * 1024
NEG = -1e30
LOG2E = math.log2(math.e)
SUM_ROWS = 16
POS_BITS = 6


def _params(*sem):
    return pltpu.CompilerParams(dimension_semantics=sem, vmem_limit_bytes=VMEM_LIMIT)


def _layer_norm(z, g, b):
    mu = jnp.mean(z, axis=-1, keepdims=True)
    zc = z - mu
    var = jnp.mean(zc * zc, axis=-1, keepdims=True)
    return zc * lax.rsqrt(var + LN_EPS) * g + b


def _dot_nt(a, b):
    return lax.dot_general(a, b, (((1,), (1,)), ((), ())), preferred_element_type=F32)


def _proj_body(x_ref, w_ref, o_ref, xb_ref, *acc_scratch, dil, head_major):
    @pl.when(pl.program_id(2) == 0)
    def _():
        xb_ref[...] = x_ref[0].astype(BF16)

    acc = jnp.dot(xb_ref[...], w_ref[...], preferred_element_type=F32)
    tm, tn = acc.shape
    if not head_major:
        o_ref[0] = acc.astype(o_ref.dtype)
    elif dil == 1:
        for c in range(tn // LANES):
            o_ref[0, c, 0] = acc[:, c * LANES:(c + 1) * LANES].astype(o_ref.dtype)
    else:
        acc_ref, = acc_scratch
        rows = tm // dil
        for c in range(tn // LANES):
            acc_ref[c] = acc[:, c * LANES:(c + 1) * LANES]
            for r in range(dil):
                o_ref[0, c, r] = acc_ref[c, pl.ds(r, rows, stride=dil), :].astype(o_ref.dtype)


def _proj(x, w, *, dil=1, head_major=True, out_dtype=BF16, tm=1024, tn=1024):
    B, S, D = x.shape
    N = w.shape[1]
    tm = min(tm, S)
    tn = min(tn, N)
    assert S % tm == 0 and N % tn == 0 and tm % dil == 0 and tn % LANES == 0
    if head_major:
        out_shape = jax.ShapeDtypeStruct((B, N // LANES, dil, S // dil, LANES), out_dtype)
        out_spec = pl.BlockSpec((1, tn // LANES, dil, tm // dil, LANES),
                                lambda b, i, j: (b, j, 0, i, 0))
    else:
        out_shape = jax.ShapeDtypeStruct((B, S, N), out_dtype)
        out_spec = pl.BlockSpec((1, tm, tn), lambda b, i, j: (b, i, j))
    scratch = [pltpu.VMEM((tm, D), BF16)]
    if head_major and dil > 1:
        scratch.append(pltpu.VMEM((tn // LANES, tm, LANES), F32))
    return pl.pallas_call(
        functools.partial(_proj_body, dil=dil, head_major=head_major),
        grid=(B, S // tm, N // tn),
        in_specs=[pl.BlockSpec((1, tm, D), lambda b, i, j: (b, i, 0)),
                  pl.BlockSpec((D, tn), lambda b, i, j: (0, j))],
        out_specs=out_spec,
        out_shape=out_shape,
        scratch_shapes=scratch,
        compiler_params=_params("parallel", "parallel", "arbitrary"),
        name="proj_in",
    )(x, w)


def _out_mlp_body(o_ref, wo_ref, x_ref, g1_ref, b1_ref, w1_ref, w2_ref, g2_ref, b2_ref, y_ref,
                  x1_ref, xb_ref, acc_ref, *, nblk):
    k = pl.program_id(2)

    @pl.when(k == 0)
    def _():
        o = jnp.concatenate([o_ref[0, c] for c in range(nblk)], axis=-1)
        y = jnp.dot(o, wo_ref[...], preferred_element_type=F32)
        x1 = _layer_norm(ALPHA * x_ref[0] + y, g1_ref[...], b1_ref[...])
        x1_ref[...] = x1
        xb_ref[...] = x1.astype(BF16)
        acc_ref[...] = jnp.zeros_like(acc_ref)

    h = jnp.dot(xb_ref[...], w1_ref[...], preferred_element_type=F32)
    h = jnp.square(jnp.maximum(h, 0.0)).astype(BF16)
    acc_ref[...] += jnp.dot(h, w2_ref[...], preferred_element_type=F32)

    @pl.when(k == pl.num_programs(2) - 1)
    def _():
        y_ref[0] = _layer_norm(ALPHA * x1_ref[...] + acc_ref[...], g2_ref[...], b2_ref[...])


def _out_mlp(o, wo, x, g1, b1, w1, w2, g2, b2, *, tm=1024, tf=512):
    B, nblk, S, wblk = o.shape
    D = x.shape[-1]
    FF = w1.shape[1]
    tm = min(tm, S)
    row = lambda a: a.reshape(1, D)
    const = lambda shape: pl.BlockSpec(shape, lambda b, i, k: (0, 0))
    return pl.pallas_call(
        functools.partial(_out_mlp_body, nblk=nblk),
        grid=(B, S // tm, FF // tf),
        in_specs=[pl.BlockSpec((1, nblk, tm, wblk), lambda b, i, k: (b, 0, i, 0)),
                  const((nblk * wblk, D)),
                  pl.BlockSpec((1, tm, D), lambda b, i, k: (b, i, 0)),
                  const((1, D)), const((1, D)),
                  pl.BlockSpec((D, tf), lambda b, i, k: (0, k)),
                  pl.BlockSpec((tf, D), lambda b, i, k: (k, 0)),
                  const((1, D)), const((1, D))],
        out_specs=pl.BlockSpec((1, tm, D), lambda b, i, k: (b, i, 0)),
        out_shape=jax.ShapeDtypeStruct(x.shape, F32),
        scratch_shapes=[pltpu.VMEM((tm, D), F32), pltpu.VMEM((tm, D), BF16),
                        pltpu.VMEM((tm, D), F32)],
        compiler_params=_params("parallel", "parallel", "arbitrary"),
        name="out_mlp_ln",
    )(o, wo, x, row(g1), row(b1), w1, w2, row(g2), row(b2))


def _diff_body(slopes_ref, feat_ref, q_ref, k_ref, v_ref, lq1_ref, lk1_ref, lq2_ref, lk2_ref, g_ref,
               o_ref, vt_ref, kx_ref, mask_ref, qm_ref, sa_ref, sb_ref, pa_ref, pb_ref, acc_ref,
               *, tk, lambda_init):
    h = pl.program_id(1)
    i = pl.program_id(2)
    d = DA_HEAD_DIM
    tq = 2 * tk
    S = k_ref.shape[3]
    lane = lax.broadcasted_iota(jnp.int32, (1, 2 * d), 1)
    digit = lane & (d - 1)
    full = slice(None)
    upper = slice(tk, tq)

    @pl.when(i == 0)
    def _():
        ch = min(512, S)
        for c in range(S // ch):
            sl = slice(c * ch, (c + 1) * ch)
            vt_ref[:2 * d, sl] = v_ref[0, 0, 0, sl, :].astype(F32).T.astype(BF16)
            feat = feat_ref[sl, :]
            k = k_ref[0, 0, 0, sl, :]
            kx_ref[0, sl, :] = jnp.where(lane < d, k, feat)
            kx_ref[1, sl, :] = jnp.where(lane >= d, k, feat)
        vt_ref[2 * d:, :] = jnp.ones((SUM_ROWS, S), BF16)
        rel = (lax.broadcasted_iota(jnp.int32, (tk, tq), 0)
               - lax.broadcasted_iota(jnp.int32, (tk, tq), 1))
        for c in range(2):
            mask_ref[c] = jnp.where(rel <= -c * tk, 0.0, NEG)

    sv = jnp.full((1, 2 * d), slopes_ref[h] * LOG2E, F32)
    s_a = sv.astype(BF16).astype(F32)
    s_b = (sv - s_a).astype(BF16).astype(F32)
    s_c = (sv - s_a - s_b).astype(BF16).astype(F32)
    piece = jnp.where((digit == 0) | (digit == 3), s_a,
                      jnp.where((digit == 1) | (digit == 4), s_b, s_c))
    qfeat = jnp.where(digit < 3, piece * (1 << POS_BITS), jnp.where(digit < 6, piece, 0.0))
    q = q_ref[0, 0, 0].astype(F32) * (d ** -0.5 * LOG2E)
    qm_ref[0] = jnp.where(lane < d, q, qfeat).astype(BF16)
    qm_ref[1] = jnp.where(lane >= d, q, qfeat).astype(BF16)
    acc_ref[...] = jnp.zeros_like(acc_ref)
    pb_ref[...] = jnp.zeros_like(pb_ref)

    def scores(t, s_ref, qs=full):
        k0 = pl.multiple_of(t * tk, tk)
        cms = []
        for mp in range(2):
            s = _dot_nt(kx_ref[mp, pl.ds(k0, tk), :], qm_ref[mp, qs, :])
            s_ref[mp, :, qs] = s
            cms.append(jnp.max(s, axis=0, keepdims=True))
        return tuple(cms)

    def values(t, p_ref, a, qs=full):
        k0 = pl.multiple_of(jnp.maximum(t, 0) * tk, tk)
        vt = vt_ref[:, pl.ds(k0, tk)]
        for mp in range(2):
            acc_ref[mp, :, qs] = a[mp] * acc_ref[mp, :, qs] + jnp.dot(
                vt, p_ref[mp, :, qs], preferred_element_type=F32)

    def softmax(s_cur, p_cur, m, cm, mask=None, qs=full):
        m_new, a_new = [], []
        for mp in range(2):
            s = s_cur[mp, :, qs]
            if mask is None:
                c = cm[mp]
            else:
                s = s + mask[:, qs]
                c = jnp.max(s, axis=0, keepdims=True)
            mn = jnp.maximum(m[mp], c)
            a_new.append(jnp.exp2(m[mp] - mn))
            p_cur[mp, :, qs] = jnp.exp2(s - mn).astype(BF16)
            m_new.append(mn)
        return tuple(m_new), tuple(a_new)

    def step(t, s_cur, s_nxt, p_cur, p_prev, carry):
        m, a, cm = carry
        m, a_new = softmax(s_cur, p_cur, m, cm)
        values(t - 1, p_prev, a)
        return m, a_new, scores(t + 1, s_nxt)

    def pair(pp, carry):
        carry = step(2 * pp, sa_ref, sb_ref, pa_ref, pb_ref, carry)
        return step(2 * pp + 1, sb_ref, sa_ref, pb_ref, pa_ref, carry)

    two = lambda x: (x, x)
    carry = (two(jnp.full((1, tq), NEG, F32)), two(jnp.ones((1, tq), F32)), scores(0, sa_ref))
    m, a, cm = lax.fori_loop(0, i, pair, carry)
    m, a_d0 = softmax(sa_ref, pa_ref, m, cm, mask=mask_ref.at[0])
    values(2 * i - 1, pb_ref, a)
    scores(2 * i + 1, sb_ref, upper)
    _, a_d1 = softmax(sb_ref, pb_ref, tuple(x[:, upper] for x in m), None,
                      mask=mask_ref.at[1], qs=upper)
    values(2 * i, pa_ref, a_d0)
    values(2 * i + 1, pb_ref, a_d1, upper)

    lam = (jnp.exp(jnp.sum(lq1_ref[...] * lk1_ref[...], axis=-1, keepdims=True))
           - jnp.exp(jnp.sum(lq2_ref[...] * lk2_ref[...], axis=-1, keepdims=True))
           + lambda_init)
    o1 = acc_ref[0, :2 * d] * (1.0 / acc_ref[0, 2 * d:2 * d + 1])
    o2 = acc_ref[1, :2 * d] * (lam / acc_ref[1, 2 * d:2 * d + 1])
    o = o1 - o2
    o = o * lax.rsqrt(jnp.mean(o * o, axis=0, keepdims=True) + RMS_EPS)
    o_ref[0, 0] = (o.T * g_ref[...] * (1.0 - lambda_init)).astype(o_ref.dtype)


def _diff_attention(qkv, lq1, lk1, lq2, lk2, subln_g, lambda_init, *, tk=256):
    B, _, _, S, _ = qkv.shape
    H, d = DA_HEADS, DA_HEAD_DIM
    tk = min(tk, S // 2)
    tq = 2 * tk
    assert S % tq == 0 and S <= 1 << (2 * POS_BITS)
    slopes = 2.0 ** (-8.0 * jnp.arange(1, H + 1, dtype=F32) / H)
    pos = jnp.arange(S, dtype=jnp.int32)[:, None]
    slot = jnp.arange(2 * d, dtype=jnp.int32)[None, :] % d
    feat = jnp.where(slot < 3, pos >> POS_BITS,
                     jnp.where(slot < 6, pos & ((1 << POS_BITS) - 1), 0)).astype(BF16)
    vec = lambda a: a.reshape(1, -1).astype(F32)
    small = lambda n: pl.BlockSpec((1, n), lambda b, h, i: (0, 0))
    scores = pltpu.VMEM((2, tk, tq), F32)
    probs = pltpu.VMEM((2, tk, tq), BF16)
    return pl.pallas_call(
        functools.partial(_diff_body, tk=tk, lambda_init=lambda_init),
        grid=(B, H, S // tq),
        in_specs=[pl.BlockSpec(memory_space=pltpu.SMEM),
                  pl.BlockSpec((S, 2 * d), lambda b, h, i: (0, 0)),
                  pl.BlockSpec((1, 1, 1, tq, 2 * d), lambda b, h, i: (b, h, 0, i, 0)),
                  pl.BlockSpec((1, 1, 1, S, 2 * d), lambda b, h, i: (b, H + h, 0, 0, 0)),
                  pl.BlockSpec((1, 1, 1, S, 2 * d), lambda b, h, i: (b, 2 * H + h, 0, 0, 0)),
                  small(d), small(d), small(d), small(d), small(2 * d)],
        out_specs=pl.BlockSpec((1, 1, tq, 2 * d), lambda b, h, i: (b, h, i, 0)),
        out_shape=jax.ShapeDtypeStruct((B, H, S, 2 * d), BF16),
        scratch_shapes=[pltpu.VMEM((2 * d + SUM_ROWS, S), BF16),
                        pltpu.VMEM((2, S, 2 * d), BF16),
                        pltpu.VMEM((2, tk, tq), F32),
                        pltpu.VMEM((2, tq, 2 * d), BF16),
                        scores, scores, probs, probs,
                        pltpu.VMEM((2, 2 * d + SUM_ROWS, tq), F32)],
        compiler_params=_params("parallel", "parallel", "arbitrary"),
        name="diff_attn",
    )(slopes, feat, qkv, qkv, qkv, vec(lq1), vec(lk1), vec(lq2), vec(lk2), vec(subln_g))


def _dil_body(slopes_ref, *refs, tt, dils):
    G = len(dils)
    in_refs, o_ref = refs[:3 * G], refs[3 * G]
    og_refs = refs[3 * G + 1:3 * G + 1 + G]
    lse_refs = refs[3 * G + 1 + G:3 * G + 1 + 2 * G]
    bias_ref = refs[3 * G + 1 + 2 * G]
    h = pl.program_id(1)
    t = pl.program_id(2)
    slope = slopes_ref[h] * LOG2E
    blk = DIL_BLOCK
    scale = DIL_HEAD_DIM ** -0.5 * LOG2E
    base = (lax.broadcasted_iota(jnp.int32, (blk, 2 * blk), 0)
            - lax.broadcasted_iota(jnp.int32, (blk, 2 * blk), 1))

    for g in range(G):
        window = DIL_GROUPS[g][0] // dils[g]
        for first in range(2):
            dist = base + (0 if first else blk)
            bias_ref[2 * g + first] = jnp.where((dist >= 0) & (dist <= window),
                                                -(slope * dils[g]) * dist.astype(F32), NEG)

    for g in range(G):
        dil = dils[g]
        q_ref, k_ref, v_ref = in_refs[3 * g:3 * g + 3]
        rows = tt // dil
        nb = rows // blk

        def block(idx, _, dil=dil, q_ref=q_ref, k_ref=k_ref, v_ref=v_ref, rows=rows, nb=nb, g=g):
            r = idx // nb
            bi = idx % nb
            lq = pl.multiple_of(bi * blk, blk)
            l0 = t * rows + lq
            ks = pl.multiple_of(jnp.maximum(l0 - blk, 0), blk)
            qb = q_ref[0, 0, r, pl.ds(lq, blk), :]
            kc = k_ref[0, 0, r, pl.ds(ks, 2 * blk), :]
            vc = v_ref[0, 0, r, pl.ds(ks, 2 * blk), :]
            s = _dot_nt(qb, kc) * scale + bias_ref[2 * g + (l0 == 0).astype(jnp.int32)]
            m = jnp.max(s, axis=-1, keepdims=True)
            p = jnp.exp2(s - m)
            l = jnp.sum(p, axis=-1, keepdims=True)
            acc = jnp.dot(p.astype(BF16), vc, preferred_element_type=F32)
            tok = pl.ds(lq * dil + r, blk, stride=dil)
            og_refs[g][tok, :] = acc * (1.0 / l)
            lse_refs[g][tok, :] = jnp.broadcast_to(m + jnp.log2(l), (blk, LANES))
            return 0

        lax.fori_loop(0, dil * nb, block, 0, unroll=DIL_UNROLL)

    ch = 256
    for c in range(tt // ch):
        sl = slice(c * ch, (c + 1) * ch)
        lses = [lse_refs[g][sl, :] for g in range(G)]
        top = functools.reduce(jnp.maximum, lses)
        ws = [jnp.exp2(lse - top) for lse in lses]
        num = sum(w * og_refs[g][sl, :] for g, w in enumerate(ws))
        o_ref[0, 0, sl, :] = (num / sum(ws)).astype(o_ref.dtype)


def _dilated_attention(qkvs, *, tt=2048):
    dils = tuple(d for _, d in DIL_GROUPS)
    B = qkvs[0].shape[0]
    S = qkvs[0].shape[2] * qkvs[0].shape[3]
    H, dh = DIL_HEADS, DIL_HEAD_DIM
    tt = min(tt, S)
    assert all(tt % (d * DIL_BLOCK) == 0 and S // d >= 2 * DIL_BLOCK for d in dils)
    slopes = 2.0 ** (-8.0 * jnp.arange(1, H + 1, dtype=F32) / H)
    in_specs = [pl.BlockSpec(memory_space=pltpu.SMEM)]
    args = [slopes]
    for a, dil in zip(qkvs, dils):
        L = S // dil
        in_specs += [
            pl.BlockSpec((1, 1, dil, tt // dil, dh), lambda b, h, t: (b, h, 0, t, 0)),
            pl.BlockSpec((1, 1, dil, L, dh), lambda b, h, t: (b, H + h, 0, 0, 0)),
            pl.BlockSpec((1, 1, dil, L, dh), lambda b, h, t: (b, 2 * H + h, 0, 0, 0))]
        args += [a, a, a]
    return pl.pallas_call(
        functools.partial(_dil_body, tt=tt, dils=dils),
        grid=(B, H, S // tt),
        in_specs=in_specs,
        out_specs=pl.BlockSpec((1, 1, tt, dh), lambda b, h, t: (b, h, t, 0)),
        out_shape=jax.ShapeDtypeStruct((B, H, S, dh), BF16),
        scratch_shapes=([pltpu.VMEM((tt, LANES), F32)] * (2 * len(dils))
                        + [pltpu.VMEM((2 * len(dils), DIL_BLOCK, 2 * DIL_BLOCK), F32)]),
        compiler_params=_params("parallel", "parallel", "arbitrary"),
        name="dilated_attn",
    )(*args)


def _gate_body(x_ref, wg_ref, w2_ref, b_ref, o_ref):
    C = GLA_CHUNK
    g_low = jnp.dot(x_ref[0].astype(BF16), wg_ref[...], preferred_element_type=F32)
    z = jnp.dot(g_low.astype(BF16), w2_ref[...], preferred_element_type=F32) + b_ref[...]
    log_a = (jnp.minimum(z, 0.0) - jnp.log1p(jnp.exp(-jnp.abs(z)))) / GLA_TAU
    tri = (lax.broadcasted_iota(jnp.int32, (C, C), 0)
           >= lax.broadcasted_iota(jnp.int32, (C, C), 1)).astype(BF16)
    for c in range(log_a.shape[0] // C):
        a = log_a[c * C:(c + 1) * C]
        hi = a.astype(BF16)
        rem = a - hi.astype(F32)
        mid = rem.astype(BF16)
        lo = (rem - mid.astype(F32)).astype(BF16)
        cum = (jnp.dot(tri, hi, preferred_element_type=F32)
               + jnp.dot(tri, mid, preferred_element_type=F32)
               + jnp.dot(tri, lo, preferred_element_type=F32))
        o_ref[0, c * C:(c + 1) * C, :] = cum


def _gla_gate(x, wg, w2, b, *, tm=512):
    B, S, D = x.shape
    tm = min(tm, S)
    return pl.pallas_call(
        _gate_body,
        grid=(B, S // tm),
        in_specs=[pl.BlockSpec((1, tm, D), lambda b_, i: (b_, i, 0)),
                  pl.BlockSpec((D, LANES), lambda b_, i: (0, 0)),
                  pl.BlockSpec((LANES, GLA_DK), lambda b_, i: (0, 0)),
                  pl.BlockSpec((1, GLA_DK), lambda b_, i: (0, 0))],
        out_specs=pl.BlockSpec((1, tm, GLA_DK), lambda b_, i: (b_, i, 0)),
        out_shape=jax.ShapeDtypeStruct((B, S, GLA_DK), F32),
        compiler_params=_params("parallel", "parallel"),
        name="gla_gate",
    )(x, wg, w2, b.reshape(1, GLA_DK))


def _gla_body(q_ref, k_ref, v_ref, r_ref, b_ref, g_ref, o_ref, st_ref):
    C = GLA_CHUNK
    dk = GLA_DK // GLA_HEADS

    @pl.when(pl.program_id(2) == 0)
    def _():
        st_ref[...] = jnp.zeros_like(st_ref)

    causal = (lax.broadcasted_iota(jnp.int32, (C, C), 0)
              >= lax.broadcasted_iota(jnp.int32, (C, C), 1))
    st = st_ref[...]
    for c in range(q_ref.shape[1] // C):
        sl = slice(c * C, (c + 1) * C)
        bc = b_ref[0, sl, :]
        bl = bc[C - 1:C, :]
        kk = k_ref[0, sl, :]
        q_dec = (q_ref[0, sl, :] * (dk ** -0.5) * jnp.exp(bc)).astype(BF16)
        k_intra = (kk * jnp.exp(-bc)).astype(BF16)
        k_state = (kk * jnp.exp(bl - bc)).astype(BF16)
        vb = v_ref[0, sl, :].astype(BF16)
        s = jnp.where(causal, _dot_nt(q_dec, k_intra), 0.0)
        o = (jnp.dot(s.astype(BF16), vb, preferred_element_type=F32)
             + _dot_nt(q_dec, st.astype(BF16)))
        st = st * jnp.exp(bl) + lax.dot_general(
            vb, k_state, (((0,), (0,)), ((), ())), preferred_element_type=F32)
        o = o * lax.rsqrt(jnp.mean(o * o, axis=-1, keepdims=True) + RMS_EPS) * g_ref[...]
        r = r_ref[0, sl, :]
        o_ref[0, 0, sl, :] = (o * (r * jax.nn.sigmoid(r))).astype(o_ref.dtype)
    st_ref[...] = st


def _gla(proj, bcum, gnorm_g, *, ts=512):
    B, S, _ = proj.shape
    H = GLA_HEADS
    dk, dv = GLA_DK // H, GLA_DV // H
    ts = min(ts, S)
    nq = GLA_DK // dk
    nv = 2 * GLA_DK // dv
    return pl.pallas_call(
        _gla_body,
        grid=(B, H, S // ts),
        in_specs=[pl.BlockSpec((1, ts, dk), lambda b, h, t: (b, t, h)),
                  pl.BlockSpec((1, ts, dk), lambda b, h, t: (b, t, nq + h)),
                  pl.BlockSpec((1, ts, dv), lambda b, h, t: (b, t, nv + h)),
                  pl.BlockSpec((1, ts, dv), lambda b, h, t: (b, t, nv + H + h)),
                  pl.BlockSpec((1, ts, dk), lambda b, h, t: (b, t, h)),
                  pl.BlockSpec((1, dv), lambda b, h, t: (0, 0))],
        out_specs=pl.BlockSpec((1, 1, ts, dv), lambda b, h, t: (b, h, t, 0)),
        out_shape=jax.ShapeDtypeStruct((B, H, S, dv), BF16),
        scratch_shapes=[pltpu.VMEM((dv, dk), F32)],
        compiler_params=_params("parallel", "parallel", "arbitrary"),
        name="gla",
    )(proj, proj, proj, proj, bcum, gnorm_g.reshape(1, dv))


def _diff_mixer(x, w_in, lq1, lk1, lq2, lk2, subln_g, layer_idx):
    lambda_init = 0.8 - 0.6 * math.exp(-0.3 * layer_idx)
    qkv = _proj(x, w_in.astype(BF16))
    return _diff_attention(qkv, lq1, lk1, lq2, lk2, subln_g, lambda_init)


def _dil_mixer(x, w_in):
    wb = w_in.astype(BF16)
    per_group = 3 * DIL_HEADS * DIL_HEAD_DIM
    qkvs = [_proj(x, wb[:, g * per_group:(g + 1) * per_group], dil=dil)
            for g, (_, dil) in enumerate(DIL_GROUPS)]
    return _dilated_attention(qkvs)


def _gla_mixer(x, w_in, w_gate2, b_gate, gnorm_g):
    n_main = 2 * GLA_DK + 2 * GLA_DV
    wb = w_in.astype(BF16)
    proj = _proj(x, wb[:, :n_main], head_major=False, out_dtype=F32)
    wg = jnp.pad(wb[:, n_main:], ((0, 0), (0, LANES - GLA_GATE_RANK)))
    w2 = jnp.pad(w_gate2.astype(BF16), ((0, LANES - GLA_GATE_RANK), (0, 0)))
    bcum = _gla_gate(x, wg, w2, b_gate)
    return _gla(proj, bcum, gnorm_g)


def _finish_layer(o, x, w_out, g1, b1, w1, w2, g2, b2):
    return _out_mlp(o, w_out.astype(BF16), x, g1, b1, w1.astype(BF16), w2.astype(BF16), g2, b2)


def kernel(x, l0_w_in, l0_lam_q1, l0_lam_k1, l0_lam_q2, l0_lam_k2, l0_subln_g, l0_w_out, l0_ln1_g, l0_ln1_b, l0_w_ff1, l0_w_ff2, l0_ln2_g, l0_ln2_b, l1_w_in, l1_w_out, l1_ln1_g, l1_ln1_b, l1_w_ff1, l1_w_ff2, l1_ln2_g, l1_ln2_b, l2_w_in, l2_w_gate2, l2_b_gate, l2_gnorm_g, l2_w_out, l2_ln1_g, l2_ln1_b, l2_w_ff1, l2_w_ff2, l2_ln2_g, l2_ln2_b, l3_w_in, l3_lam_q1, l3_lam_k1, l3_lam_q2, l3_lam_k2, l3_subln_g, l3_w_out, l3_ln1_g, l3_ln1_b, l3_w_ff1, l3_w_ff2, l3_ln2_g, l3_ln2_b):
    o = _diff_mixer(x, l0_w_in, l0_lam_q1, l0_lam_k1, l0_lam_q2, l0_lam_k2, l0_subln_g, 0)
    x = _finish_layer(o, x, l0_w_out, l0_ln1_g, l0_ln1_b, l0_w_ff1, l0_w_ff2, l0_ln2_g, l0_ln2_b)
    o = _dil_mixer(x, l1_w_in)
    x = _finish_layer(o, x, l1_w_out, l1_ln1_g, l1_ln1_b, l1_w_ff1, l1_w_ff2, l1_ln2_g, l1_ln2_b)
    o = _gla_mixer(x, l2_w_in, l2_w_gate2, l2_b_gate, l2_gnorm_g)
    x = _finish_layer(o, x, l2_w_out, l2_ln1_g, l2_ln1_b, l2_w_ff1, l2_w_ff2, l2_ln2_g, l2_ln2_b)
    o = _diff_mixer(x, l3_w_in, l3_lam_q1, l3_lam_k1, l3_lam_q2, l3_lam_k2, l3_subln_g, 3)
    x = _finish_layer(o, x, l3_w_out, l3_ln1_g, l3_ln1_b, l3_w_ff1, l3_w_ff2, l3_ln2_g, l3_ln2_b)
    return x
```

```python
import functools
import math

import jax
import jax.numpy as jnp
from jax import lax
from jax.experimental import pallas as pl
from jax.experimental.pallas import tpu as pltpu

F32 = jnp.float32
BF16 = jnp.bfloat16

D_MODEL = 1024
DEPTH = 4
N_MIXERS = 3
DA_HEADS = 8
DA_HEAD_DIM = 64
DIL_GROUPS = ((128, 1), (512, 4), (2048, 16))
DIL_HEADS = 8
DIL_HEAD_DIM = 128
DIL_BLOCK = 128
DIL_UNROLL = 16
GLA_HEADS = 4
GLA_DK = D_MODEL // 2
GLA_DV = D_MODEL
GLA_GATE_RANK = 16
GLA_TAU = 16.0
GLA_CHUNK = 64
D_FF = 4 * D_MODEL
ALPHA = (2 * DEPTH) ** 0.25
LN_EPS = 1e-5
RMS_EPS = 1e-6

LANES = 128
VMEM_LIMIT = 56 * 1024 * 1024
NEG = -1e30
LOG2E = math.log2(math.e)
SUM_ROWS = 16
LN_ROWS = 512
POS_BITS = 6


def _params(*sem):
    return pltpu.CompilerParams(dimension_semantics=sem, vmem_limit_bytes=VMEM_LIMIT)


def _layer_norm(z, g, b):
    mu = jnp.mean(z, axis=-1, keepdims=True)
    zc = z - mu
    var = jnp.mean(zc * zc, axis=-1, keepdims=True)
    return zc * lax.rsqrt(var + LN_EPS) * g + b


def _dot_nt(a, b):
    return lax.dot_general(a, b, (((1,), (1,)), ((), ())), preferred_element_type=F32)


def _proj_body(x_ref, w_ref, o_ref, xb_ref, *, head_major):
    @pl.when(pl.program_id(3) == 0)
    def _():
        xb_ref[...] = x_ref[0].astype(BF16)

    acc = jnp.dot(xb_ref[...], w_ref[...], preferred_element_type=F32)
    if head_major:
        for c in range(acc.shape[1] // LANES):
            o_ref[0, c, 0] = acc[:, c * LANES:(c + 1) * LANES].astype(o_ref.dtype)
    else:
        o_ref[0] = acc.astype(o_ref.dtype)


def _proj(x, w, *, dil=1, head_major=True, out_dtype=BF16, tm=1024, tn=1024):
    B, S, D = x.shape
    N = w.shape[1]
    L = S // dil
    tm = min(tm, L)
    if tm < 512:
        tn = N
    tn = min(tn, N)
    assert L % tm == 0 and N % tn == 0 and tn % LANES == 0 and (head_major or dil == 1)
    if head_major:
        out_shape = jax.ShapeDtypeStruct((B, N // LANES, dil, L, LANES), out_dtype)
        out_spec = pl.BlockSpec((1, tn // LANES, 1, tm, LANES),
                                lambda b, i, r, j: (b, j, r, i, 0))
    else:
        out_shape = jax.ShapeDtypeStruct((B, S, N), out_dtype)
        out_spec = pl.BlockSpec((1, tm, tn), lambda b, i, r, j: (b, i, j))
    return pl.pallas_call(
        functools.partial(_proj_body, head_major=head_major),
        grid=(B, L // tm, dil, N // tn),
        in_specs=[pl.BlockSpec((1, tm, D), lambda b, i, r, j: (b, i, r)),
                  pl.BlockSpec((D, tn), lambda b, i, r, j: (0, j))],
        out_specs=out_spec,
        out_shape=out_shape,
        scratch_shapes=[pltpu.VMEM((tm, D), BF16)],
        compiler_params=_params("parallel", "parallel", "parallel", "arbitrary"),
        name="proj_in",
    )(x.reshape(B, L, dil * D), w)


def _out_mlp_body(o_ref, wo_ref, x_ref, g1_ref, b1_ref, w1_ref, w2_ref, g2_ref, b2_ref, y_ref,
                  x1_ref, xb_ref, acc_ref, *, nblk):
    k = pl.program_id(2)
    last = pl.num_programs(2) - 1
    tm = x1_ref.shape[0]
    ch = min(LN_ROWS, tm)
    chunks = [slice(c * ch, (c + 1) * ch) for c in range(tm // ch)]

    def hidden(xb):
        h = jnp.dot(xb, w1_ref[...], preferred_element_type=F32)
        h = jnp.square(jnp.maximum(h, 0.0)).astype(BF16)
        return jnp.dot(h, w2_ref[...], preferred_element_type=F32)

    @pl.when(k == 0)
    def _():
        for rows in chunks:
            o = jnp.concatenate([o_ref[0, n, rows, :] for n in range(nblk)], axis=-1)
            y = jnp.dot(o, wo_ref[...], preferred_element_type=F32)
            x1 = _layer_norm(ALPHA * x_ref[0, rows, :] + y, g1_ref[...], b1_ref[...])
            x1_ref[rows, :] = x1
            xb = x1.astype(BF16)
            xb_ref[rows, :] = xb
            acc_ref[rows, :] = hidden(xb)

    @pl.when((k > 0) & (k < last))
    def _():
        acc_ref[...] += hidden(xb_ref[...])

    @pl.when(k == last)
    def _():
        for rows in chunks:
            z = ALPHA * x1_ref[rows, :] + (acc_ref[rows, :] + hidden(xb_ref[rows, :]))
            y_ref[0, rows, :] = _layer_norm(z, g2_ref[...], b2_ref[...])


def _out_mlp(o, wo, x, g1, b1, w1, w2, g2, b2, *, tm=1024, tf=512):
    B, nblk, S, wblk = o.shape
    D = x.shape[-1]
    FF = w1.shape[1]
    tm = min(tm, S)
    assert FF // tf >= 2
    row = lambda a: a.reshape(1, D)
    const = lambda shape: pl.BlockSpec(shape, lambda b, i, k: (0, 0))
    return pl.pallas_call(
        functools.partial(_out_mlp_body, nblk=nblk),
        grid=(B, S // tm, FF // tf),
        in_specs=[pl.BlockSpec((1, nblk, tm, wblk), lambda b, i, k: (b, 0, i, 0)),
                  const((nblk * wblk, D)),
                  pl.BlockSpec((1, tm, D), lambda b, i, k: (b, i, 0)),
                  const((1, D)), const((1, D)),
                  pl.BlockSpec((D, tf), lambda b, i, k: (0, k)),
                  pl.BlockSpec((tf, D), lambda b, i, k: (k, 0)),
                  const((1, D)), const((1, D))],
        out_specs=pl.BlockSpec((1, tm, D), lambda b, i, k: (b, i, 0)),
        out_shape=jax.ShapeDtypeStruct(x.shape, F32),
        scratch_shapes=[pltpu.VMEM((tm, D), F32), pltpu.VMEM((tm, D), BF16),
                        pltpu.VMEM((tm, D), F32)],
        compiler_params=_params("parallel", "parallel", "arbitrary"),
        name="out_mlp_ln",
    )(o, wo, x, row(g1), row(b1), w1, w2, row(g2), row(b2))


def _diff_body(slopes_ref, feat_ref, q_ref, k_ref, v_ref, lq1_ref, lk1_ref, lq2_ref, lk2_ref, g_ref,
               o_ref, vt_ref, kx_ref, mask_ref, qm_ref, sa_ref, sb_ref, pa_ref, pb_ref, acc_ref,
               *, tk, lambda_init):
    h = pl.program_id(1)
    i = pl.program_id(2)
    d = DA_HEAD_DIM
    tq = 2 * tk
    S = k_ref.shape[3]
    lane = lax.broadcasted_iota(jnp.int32, (1, 2 * d), 1)
    digit = lane & (d - 1)
    full = slice(None)
    upper = slice(tk, tq)

    @pl.when(i == 0)
    def _():
        ch = min(512, S)
        for c in range(S // ch):
            sl = slice(c * ch, (c + 1) * ch)
            vt_ref[:2 * d, sl] = v_ref[0, 0, 0, sl, :].astype(F32).T.astype(BF16)
            feat = feat_ref[sl, :]
            k = k_ref[0, 0, 0, sl, :]
            kx_ref[0, sl, :] = jnp.where(lane < d, k, feat)
            kx_ref[1, sl, :] = jnp.where(lane >= d, k, feat)
        vt_ref[2 * d:, :] = jnp.ones((SUM_ROWS, S), BF16)
        rel = (lax.broadcasted_iota(jnp.int32, (tk, tq), 0)
               - lax.broadcasted_iota(jnp.int32, (tk, tq), 1))
        for c in range(2):
            mask_ref[c] = jnp.where(rel <= -c * tk, 0.0, NEG)

    sv = jnp.full((1, 2 * d), slopes_ref[h] * LOG2E, F32)
    s_a = sv.astype(BF16).astype(F32)
    s_b = (sv - s_a).astype(BF16).astype(F32)
    s_c = (sv - s_a - s_b).astype(BF16).astype(F32)
    piece = jnp.where((digit == 0) | (digit == 3), s_a,
                      jnp.where((digit == 1) | (digit == 4), s_b, s_c))
    qfeat = jnp.where(digit < 3, piece * (1 << POS_BITS), jnp.where(digit < 6, piece, 0.0))
    q = q_ref[0, 0, 0].astype(F32) * (d ** -0.5 * LOG2E)
    qm_ref[0] = jnp.where(lane < d, q, qfeat).astype(BF16)
    qm_ref[1] = jnp.where(lane >= d, q, qfeat).astype(BF16)
    acc_ref[...] = jnp.zeros_like(acc_ref)
    pb_ref[...] = jnp.zeros_like(pb_ref)

    def scores(t, s_ref, qs=full):
        k0 = pl.multiple_of(t * tk, tk)
        cms = []
        for mp in range(2):
            s = _dot_nt(kx_ref[mp, pl.ds(k0, tk), :], qm_ref[mp, qs, :])
            s_ref[mp, :, qs] = s
            cms.append(jnp.max(s, axis=0, keepdims=True))
        return tuple(cms)

    def values(t, p_ref, a, qs=full):
        k0 = pl.multiple_of(t * tk, tk)
        vt = vt_ref[:, pl.ds(k0, tk)]
        for mp in range(2):
            acc_ref[mp, :, qs] = a[mp] * acc_ref[mp, :, qs] + jnp.dot(
                vt, p_ref[mp, :, qs], preferred_element_type=F32)

    def softmax(s_cur, p_cur, m, cm, mask=None, qs=full):
        m_new, a_new = [], []
        for mp in range(2):
            s = s_cur[mp, :, qs]
            if mask is None:
                c = cm[mp]
            else:
                s = s + mask[:, qs]
                c = jnp.max(s, axis=0, keepdims=True)
            mn = jnp.maximum(m[mp], c)
            a_new.append(jnp.exp2(m[mp] - mn))
            p_cur[mp, :, qs] = jnp.exp2(s - mn).astype(BF16)
            m_new.append(mn)
        return tuple(m_new), tuple(a_new)

    def step(t, s_cur, s_nxt, p_cur, p_prev, carry):
        m, a, cm = carry
        m, a_new = softmax(s_cur, p_cur, m, cm)
        values(jnp.where(t == 0, last, t - 1), p_prev, a)
        return m, a_new, scores(t + 1, s_nxt)

    def pair(pp, carry):
        carry = step(2 * pp, sa_ref, sb_ref, pa_ref, pb_ref, carry)
        return step(2 * pp + 1, sb_ref, sa_ref, pb_ref, pa_ref, carry)

    two = lambda x: (x, x)
    last = 2 * i + 1
    cm = scores(0, sa_ref)
    scores(last, sb_ref, upper)
    never = jnp.full((1, tk), NEG, F32)
    m_up, _ = softmax(sb_ref, pb_ref, two(never), None, mask=mask_ref.at[1], qs=upper)
    m = tuple(jnp.concatenate([never, x], axis=1) for x in m_up)
    carry = (m, two(jnp.ones((1, tq), F32)), cm)
    m, a, cm = lax.fori_loop(0, i, pair, carry)
    m, a_d0 = softmax(sa_ref, pa_ref, m, cm, mask=mask_ref.at[0])
    values(jnp.where(i == 0, last, 2 * i - 1), pb_ref, a)
    values(2 * i, pa_ref, a_d0)

    lam = (jnp.exp(jnp.sum(lq1_ref[...] * lk1_ref[...], axis=-1, keepdims=True))
           - jnp.exp(jnp.sum(lq2_ref[...] * lk2_ref[...], axis=-1, keepdims=True))
           + lambda_init)
    o1 = acc_ref[0, :2 * d] * (1.0 / acc_ref[0, 2 * d:2 * d + 1])
    o2 = acc_ref[1, :2 * d] * (lam / acc_ref[1, 2 * d:2 * d + 1])
    o = o1 - o2
    o = o * lax.rsqrt(jnp.mean(o * o, axis=0, keepdims=True) + RMS_EPS)
    o_ref[0, 0] = (o.T * g_ref[...] * (1.0 - lambda_init)).astype(o_ref.dtype)


def _diff_attention(qkv, lq1, lk1, lq2, lk2, subln_g, lambda_init, *, tk=256):
    B, _, _, S, _ = qkv.shape
    H, d = DA_HEADS, DA_HEAD_DIM
    tk = min(tk, S // 2)
    tq = 2 * tk
    assert S % tq == 0 and S <= 1 << (2 * POS_BITS)
    slopes = 2.0 ** (-8.0 * jnp.arange(1, H + 1, dtype=F32) / H)
    pos = jnp.arange(S, dtype=jnp.int32)[:, None]
    slot = jnp.arange(2 * d, dtype=jnp.int32)[None, :] % d
    feat = jnp.where(slot < 3, pos >> POS_BITS,
                     jnp.where(slot < 6, pos & ((1 << POS_BITS) - 1), 0)).astype(BF16)
    vec = lambda a: a.reshape(1, -1).astype(F32)
    small = lambda n: pl.BlockSpec((1, n), lambda b, h, i: (0, 0))
    scores = pltpu.VMEM((2, tk, tq), F32)
    probs = pltpu.VMEM((2, tk, tq), BF16)
    return pl.pallas_call(
        functools.partial(_diff_body, tk=tk, lambda_init=lambda_init),
        grid=(B, H, S // tq),
        in_specs=[pl.BlockSpec(memory_space=pltpu.SMEM),
                  pl.BlockSpec((S, 2 * d), lambda b, h, i: (0, 0)),
                  pl.BlockSpec((1, 1, 1, tq, 2 * d), lambda b, h, i: (b, h, 0, i, 0)),
                  pl.BlockSpec((1, 1, 1, S, 2 * d), lambda b, h, i: (b, H + h, 0, 0, 0)),
                  pl.BlockSpec((1, 1, 1, S, 2 * d), lambda b, h, i: (b, 2 * H + h, 0, 0, 0)),
                  small(d), small(d), small(d), small(d), small(2 * d)],
        out_specs=pl.BlockSpec((1, 1, tq, 2 * d), lambda b, h, i: (b, h, i, 0)),
        out_shape=jax.ShapeDtypeStruct((B, H, S, 2 * d), BF16),
        scratch_shapes=[pltpu.VMEM((2 * d + SUM_ROWS, S), BF16),
                        pltpu.VMEM((2, S, 2 * d), BF16),
                        pltpu.VMEM((2, tk, tq), F32),
                        pltpu.VMEM((2, tq, 2 * d), BF16),
                        scores, scores, probs, probs,
                        pltpu.VMEM((2, 2 * d + SUM_ROWS, tq), F32)],
        compiler_params=_params("parallel", "parallel", "arbitrary"),
        name="diff_attn",
    )(slopes, feat, qkv, qkv, qkv, vec(lq1), vec(lk1), vec(lq2), vec(lk2), vec(subln_g))


def _dil_body(slopes_ref, *refs, tt, dils):
    G = len(dils)
    in_refs, o_ref = refs[:3 * G], refs[3 * G]
    og_refs = refs[3 * G + 1:3 * G + 1 + G]
    lse_refs = refs[3 * G + 1 + G:3 * G + 1 + 2 * G]
    bias_ref = refs[3 * G + 1 + 2 * G]
    h = pl.program_id(1)
    t = pl.program_id(2)
    slope = slopes_ref[h] * LOG2E
    blk = DIL_BLOCK
    scale = DIL_HEAD_DIM ** -0.5 * LOG2E
    base = (lax.broadcasted_iota(jnp.int32, (blk, 2 * blk), 0)
            - lax.broadcasted_iota(jnp.int32, (blk, 2 * blk), 1))

    for g in range(G):
        window = DIL_GROUPS[g][0] // dils[g]
        for first in range(2):
            dist = base + (0 if first else blk)
            bias_ref[2 * g + first] = jnp.where((dist >= 0) & (dist <= window),
                                                -(slope * dils[g]) * dist.astype(F32), NEG)

    for g in range(G):
        dil = dils[g]
        q_ref, k_ref, v_ref = in_refs[3 * g:3 * g + 3]
        rows = tt // dil
        nb = rows // blk

        def block(idx, _, dil=dil, q_ref=q_ref, k_ref=k_ref, v_ref=v_ref, rows=rows, nb=nb, g=g):
            r = idx // nb
            bi = idx % nb
            lq = pl.multiple_of(bi * blk, blk)
            l0 = t * rows + lq
            ks = pl.multiple_of(jnp.maximum(l0 - blk, 0), blk)
            qb = q_ref[0, 0, r, pl.ds(lq, blk), :]
            kc = k_ref[0, 0, r, pl.ds(ks, 2 * blk), :]
            vc = v_ref[0, 0, r, pl.ds(ks, 2 * blk), :]
            s = _dot_nt(qb, kc) * scale + bias_ref[2 * g + (l0 == 0).astype(jnp.int32)]
            m = jnp.max(s, axis=-1, keepdims=True)
            p = jnp.exp2(s - m)
            l = jnp.sum(p, axis=-1, keepdims=True)
            acc = jnp.dot(p.astype(BF16), vc, preferred_element_type=F32)
            tok = pl.ds(lq * dil + r, blk, stride=dil)
            og_refs[g][tok, :] = acc * (1.0 / l)
            lse_refs[g][tok, :] = jnp.broadcast_to(m + jnp.log2(l), (blk, LANES))
            return 0

        lax.fori_loop(0, dil * nb, block, 0, unroll=DIL_UNROLL)

    ch = 256
    for c in range(tt // ch):
        sl = slice(c * ch, (c + 1) * ch)
        lses = [lse_refs[g][sl, :] for g in range(G)]
        top = functools.reduce(jnp.maximum, lses)
        ws = [jnp.exp2(lse - top) for lse in lses]
        num = sum(w * og_refs[g][sl, :] for g, w in enumerate(ws))
        o_ref[0, 0, sl, :] = (num / sum(ws)).astype(o_ref.dtype)


def _dilated_attention(qkvs, *, tt=2048):
    dils = tuple(d for _, d in DIL_GROUPS)
    B = qkvs[0].shape[0]
    S = qkvs[0].shape[2] * qkvs[0].shape[3]
    H, dh = DIL_HEADS, DIL_HEAD_DIM
    tt = min(tt, S)
    assert all(tt % (d * DIL_BLOCK) == 0 and S // d >= 2 * DIL_BLOCK for d in dils)
    slopes = 2.0 ** (-8.0 * jnp.arange(1, H + 1, dtype=F32) / H)
    in_specs = [pl.BlockSpec(memory_space=pltpu.SMEM)]
    args = [slopes]
    for a, dil in zip(qkvs, dils):
        L = S // dil
        in_specs += [
            pl.BlockSpec((1, 1, dil, tt // dil, dh), lambda b, h, t: (b, h, 0, t, 0)),
            pl.BlockSpec((1, 1, dil, L, dh), lambda b, h, t: (b, H + h, 0, 0, 0)),
            pl.BlockSpec((1, 1, dil, L, dh), lambda b, h, t: (b, 2 * H + h, 0, 0, 0))]
        args += [a, a, a]
    return pl.pallas_call(
        functools.partial(_dil_body, tt=tt, dils=dils),
        grid=(B, H, S // tt),
        in_specs=in_specs,
        out_specs=pl.BlockSpec((1, 1, tt, dh), lambda b, h, t: (b, h, t, 0)),
        out_shape=jax.ShapeDtypeStruct((B, H, S, dh), BF16),
        scratch_shapes=([pltpu.VMEM((tt, LANES), F32)] * (2 * len(dils))
                        + [pltpu.VMEM((2 * len(dils), DIL_BLOCK, 2 * DIL_BLOCK), F32)]),
        compiler_params=_params("parallel", "parallel", "arbitrary"),
        name="dilated_attn",
    )(*args)


def _gate_body(x_ref, wg_ref, w2_ref, b_ref, o_ref):
    C = GLA_CHUNK
    g_low = jnp.dot(x_ref[0].astype(BF16), wg_ref[...], preferred_element_type=F32)
    z = jnp.dot(g_low.astype(BF16), w2_ref[...], preferred_element_type=F32) + b_ref[...]
    log_a = (jnp.minimum(z, 0.0) - jnp.log1p(jnp.exp(-jnp.abs(z)))) / GLA_TAU
    tri = (lax.broadcasted_iota(jnp.int32, (C, C), 0)
           >= lax.broadcasted_iota(jnp.int32, (C, C), 1)).astype(BF16)
    for c in range(log_a.shape[0] // C):
        a = log_a[c * C:(c + 1) * C]
        hi = a.astype(BF16)
        rem = a - hi.astype(F32)
        mid = rem.astype(BF16)
        lo = (rem - mid.astype(F32)).astype(BF16)
        cum = (jnp.dot(tri, hi, preferred_element_type=F32)
               + jnp.dot(tri, mid, preferred_element_type=F32)
               + jnp.dot(tri, lo, preferred_element_type=F32))
        o_ref[0, c * C:(c + 1) * C, :] = cum


def _gla_gate(x, wg, w2, b, *, tm=512):
    B, S, D = x.shape
    tm = min(tm, S)
    return pl.pallas_call(
        _gate_body,
        grid=(B, S // tm),
        in_specs=[pl.BlockSpec((1, tm, D), lambda b_, i: (b_, i, 0)),
                  pl.BlockSpec((D, LANES), lambda b_, i: (0, 0)),
                  pl.BlockSpec((LANES, GLA_DK), lambda b_, i: (0, 0)),
                  pl.BlockSpec((1, GLA_DK), lambda b_, i: (0, 0))],
        out_specs=pl.BlockSpec((1, tm, GLA_DK), lambda b_, i: (b_, i, 0)),
        out_shape=jax.ShapeDtypeStruct((B, S, GLA_DK), F32),
        compiler_params=_params("parallel", "parallel"),
        name="gla_gate",
    )(x, wg, w2, b.reshape(1, GLA_DK))


def _gla_body(q_ref, k_ref, v_ref, r_ref, b_ref, g_ref, o_ref, st_ref, *, hp):
    C = GLA_CHUNK
    dk, dv = GLA_DK // GLA_HEADS, GLA_DV // GLA_HEADS

    @pl.when(pl.program_id(2) == 0)
    def _():
        st_ref[...] = jnp.zeros_like(st_ref)

    causal = (lax.broadcasted_iota(jnp.int32, (C, C), 0)
              >= lax.broadcasted_iota(jnp.int32, (C, C), 1))
    st = [st_ref[hh] for hh in range(hp)]
    for c in range(q_ref.shape[1] // C):
        sl = slice(c * C, (c + 1) * C)
        for hh in range(hp):
            ck = slice(hh * dk, (hh + 1) * dk)
            cv = slice(hh * dv, (hh + 1) * dv)
            bc = b_ref[0, sl, ck]
            bl = bc[C - 1:C, :]
            kk = k_ref[0, sl, ck]
            q_dec = (q_ref[0, sl, ck] * (dk ** -0.5) * jnp.exp(bc)).astype(BF16)
            k_intra = (kk * jnp.exp(-bc)).astype(BF16)
            k_state = (kk * jnp.exp(bl - bc)).astype(BF16)
            vb = v_ref[0, sl, cv].astype(BF16)
            s = jnp.where(causal, _dot_nt(q_dec, k_intra), 0.0)
            o = (jnp.dot(s.astype(BF16), vb, preferred_element_type=F32)
                 + _dot_nt(q_dec, st[hh].astype(BF16)))
            st[hh] = st[hh] * jnp.exp(bl) + lax.dot_general(
                vb, k_state, (((0,), (0,)), ((), ())), preferred_element_type=F32)
            o = o * lax.rsqrt(jnp.mean(o * o, axis=-1, keepdims=True) + RMS_EPS) * g_ref[...]
            r = r_ref[0, sl, cv]
            o_ref[0, hh, sl, :] = (o * (r * jax.nn.sigmoid(r))).astype(o_ref.dtype)
    for hh in range(hp):
        st_ref[hh] = st[hh]


def _gla(proj, bcum, gnorm_g, *, ts=512, hp=4):
    B, S, _ = proj.shape
    H = GLA_HEADS
    dk, dv = GLA_DK // H, GLA_DV // H
    ts = min(ts, S)
    wk, wv = hp * dk, hp * dv
    nk = GLA_DK // wk
    nv = 2 * GLA_DK // wv
    return pl.pallas_call(
        functools.partial(_gla_body, hp=hp),
        grid=(B, H // hp, S // ts),
        in_specs=[pl.BlockSpec((1, ts, wk), lambda b, h, t: (b, t, h)),
                  pl.BlockSpec((1, ts, wk), lambda b, h, t: (b, t, nk + h)),
                  pl.BlockSpec((1, ts, wv), lambda b, h, t: (b, t, nv + h)),
                  pl.BlockSpec((1, ts, wv), lambda b, h, t: (b, t, nv + H // hp + h)),
                  pl.BlockSpec((1, ts, wk), lambda b, h, t: (b, t, h)),
                  pl.BlockSpec((1, dv), lambda b, h, t: (0, 0))],
        out_specs=pl.BlockSpec((1, hp, ts, dv), lambda b, h, t: (b, h, t, 0)),
        out_shape=jax.ShapeDtypeStruct((B, H, S, dv), BF16),
        scratch_shapes=[pltpu.VMEM((hp, dv, dk), F32)],
        compiler_params=_params("parallel", "parallel", "arbitrary"),
        name="gla",
    )(proj, proj, proj, proj, bcum, gnorm_g.reshape(1, dv))


def _diff_mixer(x, w_in, lq1, lk1, lq2, lk2, subln_g, layer_idx):
    lambda_init = 0.8 - 0.6 * math.exp(-0.3 * layer_idx)
    qkv = _proj(x, w_in.astype(BF16))
    return _diff_attention(qkv, lq1, lk1, lq2, lk2, subln_g, lambda_init)


def _dil_mixer(x, w_in):
    wb = w_in.astype(BF16)
    per_group = 3 * DIL_HEADS * DIL_HEAD_DIM
    qkvs = [_proj(x, wb[:, g * per_group:(g + 1) * per_group], dil=dil)
            for g, (_, dil) in enumerate(DIL_GROUPS)]
    return _dilated_attention(qkvs)


def _gla_mixer(x, w_in, w_gate2, b_gate, gnorm_g):
    n_main = 2 * GLA_DK + 2 * GLA_DV
    wb = w_in.astype(BF16)
    proj = _proj(x, wb[:, :n_main], head_major=False, out_dtype=F32)
    wg = jnp.pad(wb[:, n_main:], ((0, 0), (0, LANES - GLA_GATE_RANK)))
    w2 = jnp.pad(w_gate2.astype(BF16), ((0, LANES - GLA_GATE_RANK), (0, 0)))
    bcum = _gla_gate(x, wg, w2, b_gate)
    return _gla(proj, bcum, gnorm_g)


def _finish_layer(o, x, w_out, g1, b1, w1, w2, g2, b2):
    return _out_mlp(o, w_out.astype(BF16), x, g1, b1, w1.astype(BF16), w2.astype(BF16), g2, b2)


def kernel(x, l0_w_in, l0_lam_q1, l0_lam_k1, l0_lam_q2, l0_lam_k2, l0_subln_g, l0_w_out, l0_ln1_g, l0_ln1_b, l0_w_ff1, l0_w_ff2, l0_ln2_g, l0_ln2_b, l1_w_in, l1_w_out, l1_ln1_g, l1_ln1_b, l1_w_ff1, l1_w_ff2, l1_ln2_g, l1_ln2_b, l2_w_in, l2_w_gate2, l2_b_gate, l2_gnorm_g, l2_w_out, l2_ln1_g, l2_ln1_b, l2_w_ff1, l2_w_ff2, l2_ln2_g, l2_ln2_b, l3_w_in, l3_lam_q1, l3_lam_k1, l3_lam_q2, l3_lam_k2, l3_subln_g, l3_w_out, l3_ln1_g, l3_ln1_b, l3_w_ff1, l3_w_ff2, l3_ln2_g, l3_ln2_b):
    o = _diff_mixer(x, l0_w_in, l0_lam_q1, l0_lam_k1, l0_lam_q2, l0_lam_k2, l0_subln_g, 0)
    x = _finish_layer(o, x, l0_w_out, l0_ln1_g, l0_ln1_b, l0_w_ff1, l0_w_ff2, l0_ln2_g, l0_ln2_b)
    o = _dil_mixer(x, l1_w_in)
    x = _finish_layer(o, x, l1_w_out, l1_ln1_g, l1_ln1_b, l1_w_ff1, l1_w_ff2, l1_ln2_g, l1_ln2_b)
    o = _gla_mixer(x, l2_w_in, l2_w_gate2, l2_b_gate, l2_gnorm_g)
    x = _finish_layer(o, x, l2_w_out, l2_ln1_g, l2_ln1_b, l2_w_ff1, l2_w_ff2, l2_ln2_g, l2_ln2_b)
    o = _diff_mixer(x, l3_w_in, l3_lam_q1, l3_lam_k1, l3_lam_q2, l3_lam_k2, l3_subln_g, 3)
    x = _finish_layer(o, x, l3_w_out, l3_ln1_g, l3_ln1_b, l3_w_ff1, l3_w_ff2, l3_ln2_g, l3_ln2_b)
    return x
```

```python
import functools
import math

import jax
import jax.numpy as jnp
from jax import lax
from jax.experimental import pallas as pl
from jax.experimental.pallas import tpu as pltpu

F32 = jnp.float32
BF16 = jnp.bfloat16

D_MODEL = 1024
DEPTH = 4
N_MIXERS = 3
DA_HEADS = 8
DA_HEAD_DIM = 64
DIL_GROUPS = ((128, 1), (512, 4), (2048, 16))
DIL_HEADS = 8
DIL_HEAD_DIM = 128
DIL_BLOCK = 128
DIL_UNROLL = 16
GLA_HEADS = 4
GLA_DK = D_MODEL // 2
GLA_DV = D_MODEL
GLA_GATE_RANK = 16
GLA_TAU = 16.0
GLA_CHUNK = 64
D_FF = 4 * D_MODEL
ALPHA = (2 * DEPTH) ** 0.25
LN_EPS = 1e-5
RMS_EPS = 1e-6

LANES = 128
VMEM_LIMIT = 56 * 1024 * 1024
NEG = -1e30
LOG2E = math.log2(math.e)
SUM_ROWS = 16
LN_ROWS = 512
POS_BITS = 6


def _params(*sem):
    return pltpu.CompilerParams(dimension_semantics=sem, vmem_limit_bytes=VMEM_LIMIT)


def _layer_norm(z, g, b):
    mu = jnp.mean(z, axis=-1, keepdims=True)
    zc = z - mu
    var = jnp.mean(zc * zc, axis=-1, keepdims=True)
    return zc * lax.rsqrt(var + LN_EPS) * g + b


def _dot_nt(a, b):
    return lax.dot_general(a, b, (((1,), (1,)), ((), ())), preferred_element_type=F32)


def _proj_body(x_ref, w_ref, o_ref, xb_ref, *acc_scratch, dil, head_major):
    @pl.when(pl.program_id(2) == 0)
    def _():
        xb_ref[...] = x_ref[0].astype(BF16)

    acc = jnp.dot(xb_ref[...], w_ref[...], preferred_element_type=F32)
    tm, tn = acc.shape
    if not head_major:
        o_ref[0] = acc.astype(o_ref.dtype)
    elif dil == 1:
        for c in range(tn // LANES):
            o_ref[0, c, 0] = acc[:, c * LANES:(c + 1) * LANES].astype(o_ref.dtype)
    else:
        acc_ref, = acc_scratch
        rows = tm // dil
        for c in range(tn // LANES):
            acc_ref[c] = acc[:, c * LANES:(c + 1) * LANES]
            for r in range(dil):
                o_ref[0, c, r] = acc_ref[c, pl.ds(r, rows, stride=dil), :].astype(o_ref.dtype)


def _proj(x, w, *, dil=1, head_major=True, out_dtype=BF16, tm=1024, tn=1024):
    B, S, D = x.shape
    N = w.shape[1]
    tm = min(tm, S)
    tn = min(tn, N)
    assert S % tm == 0 and N % tn == 0 and tm % dil == 0 and tn % LANES == 0
    if head_major:
        out_shape = jax.ShapeDtypeStruct((B, N // LANES, dil, S // dil, LANES), out_dtype)
        out_spec = pl.BlockSpec((1, tn // LANES, dil, tm // dil, LANES),
                                lambda b, i, j: (b, j, 0, i, 0))
    else:
        out_shape = jax.ShapeDtypeStruct((B, S, N), out_dtype)
        out_spec = pl.BlockSpec((1, tm, tn), lambda b, i, j: (b, i, j))
    scratch = [pltpu.VMEM((tm, D), BF16)]
    if head_major and dil > 1:
        scratch.append(pltpu.VMEM((tn // LANES, tm, LANES), F32))
    return pl.pallas_call(
        functools.partial(_proj_body, dil=dil, head_major=head_major),
        grid=(B, S // tm, N // tn),
        in_specs=[pl.BlockSpec((1, tm, D), lambda b, i, j: (b, i, 0)),
                  pl.BlockSpec((D, tn), lambda b, i, j: (0, j))],
        out_specs=out_spec,
        out_shape=out_shape,
        scratch_shapes=scratch,
        compiler_params=_params("parallel", "parallel", "arbitrary"),
        name="proj_in",
    )(x, w)


def _out_mlp_body(o_ref, wo_ref, x_ref, g1_ref, b1_ref, w1_ref, w2_ref, g2_ref, b2_ref, y_ref,
                  x1_ref, xb_ref, acc_ref, *, nblk):
    k = pl.program_id(2)
    last = pl.num_programs(2) - 1
    tm = x1_ref.shape[0]
    ch = min(LN_ROWS, tm)
    chunks = [slice(c * ch, (c + 1) * ch) for c in range(tm // ch)]

    def hidden(xb):
        h = jnp.dot(xb, w1_ref[...], preferred_element_type=F32)
        h = jnp.square(jnp.maximum(h, 0.0)).astype(BF16)
        return jnp.dot(h, w2_ref[...], preferred_element_type=F32)

    @pl.when(k == 0)
    def _():
        for rows in chunks:
            o = jnp.concatenate([o_ref[0, n, rows, :] for n in range(nblk)], axis=-1)
            y = jnp.dot(o, wo_ref[...], preferred_element_type=F32)
            x1 = _layer_norm(ALPHA * x_ref[0, rows, :] + y, g1_ref[...], b1_ref[...])
            x1_ref[rows, :] = x1
            xb = x1.astype(BF16)
            xb_ref[rows, :] = xb
            acc_ref[rows, :] = hidden(xb)

    @pl.when((k > 0) & (k < last))
    def _():
        acc_ref[...] += hidden(xb_ref[...])

    @pl.when(k == last)
    def _():
        for rows in chunks:
            z = ALPHA * x1_ref[rows, :] + (acc_ref[rows, :] + hidden(xb_ref[rows, :]))
            y_ref[0, rows, :] = _layer_norm(z, g2_ref[...], b2_ref[...])


def _out_mlp(o, wo, x, g1, b1, w1, w2, g2, b2, *, tm=1024, tf=1024):
    B, nblk, S, wblk = o.shape
    D = x.shape[-1]
    FF = w1.shape[1]
    tm = min(tm, S)
    assert FF // tf >= 2
    row = lambda a: a.reshape(1, D)
    const = lambda shape: pl.BlockSpec(shape, lambda b, i, k: (0, 0))
    return pl.pallas_call(
        functools.partial(_out_mlp_body, nblk=nblk),
        grid=(B, S // tm, FF // tf),
        in_specs=[pl.BlockSpec((1, nblk, tm, wblk), lambda b, i, k: (b, 0, i, 0)),
                  const((nblk * wblk, D)),
                  pl.BlockSpec((1, tm, D), lambda b, i, k: (b, i, 0)),
                  const((1, D)), const((1, D)),
                  pl.BlockSpec((D, tf), lambda b, i, k: (0, k)),
                  pl.BlockSpec((tf, D), lambda b, i, k: (k, 0)),
                  const((1, D)), const((1, D))],
        out_specs=pl.BlockSpec((1, tm, D), lambda b, i, k: (b, i, 0)),
        out_shape=jax.ShapeDtypeStruct(x.shape, F32),
        scratch_shapes=[pltpu.VMEM((tm, D), F32), pltpu.VMEM((tm, D), BF16),
                        pltpu.VMEM((tm, D), F32)],
        compiler_params=_params("parallel", "parallel", "arbitrary"),
        name="out_mlp_ln",
    )(o, wo, x, row(g1), row(b1), w1, w2, row(g2), row(b2))


def _diff_body(slopes_ref, feat_ref, q_ref, k_ref, v_ref, lq1_ref, lk1_ref, lq2_ref, lk2_ref, g_ref,
               o_ref, vt_ref, kx_ref, mask_ref, qm_ref, sa_ref, sb_ref, pa_ref, pb_ref, acc_ref,
               *, tk, lambda_init):
    h = pl.program_id(1)
    d = DA_HEAD_DIM
    tq = 2 * tk
    S = k_ref.shape[3]
    lane = lax.broadcasted_iota(jnp.int32, (1, 2 * d), 1)
    digit = lane & (d - 1)
    full = slice(None)
    upper = slice(tk, tq)

    ch = min(512, S)
    for c in range(S // ch):
        sl = slice(c * ch, (c + 1) * ch)
        vt_ref[:2 * d, sl] = v_ref[0, 0, 0, sl, :].astype(F32).T.astype(BF16)
        feat = feat_ref[sl, :]
        k = k_ref[0, 0, 0, sl, :]
        kx_ref[0, sl, :] = jnp.where(lane < d, k, feat)
        kx_ref[1, sl, :] = jnp.where(lane >= d, k, feat)
    vt_ref[2 * d:, :] = jnp.ones((SUM_ROWS, S), BF16)
    rel = (lax.broadcasted_iota(jnp.int32, (tk, tq), 0)
           - lax.broadcasted_iota(jnp.int32, (tk, tq), 1))
    for c in range(2):
        mask_ref[c] = jnp.where(rel <= -c * tk, 0.0, NEG)

    sv = jnp.full((1, 2 * d), slopes_ref[h] * LOG2E, F32)
    s_a = sv.astype(BF16).astype(F32)
    s_b = (sv - s_a).astype(BF16).astype(F32)
    s_c = (sv - s_a - s_b).astype(BF16).astype(F32)
    piece = jnp.where((digit == 0) | (digit == 3), s_a,
                      jnp.where((digit == 1) | (digit == 4), s_b, s_c))
    qfeat = jnp.where(digit < 3, piece * (1 << POS_BITS), jnp.where(digit < 6, piece, 0.0))
    lam = (jnp.exp(jnp.sum(lq1_ref[...] * lk1_ref[...], axis=-1, keepdims=True))
           - jnp.exp(jnp.sum(lq2_ref[...] * lk2_ref[...], axis=-1, keepdims=True))
           + lambda_init)

    def scores(t, s_ref, qs=full):
        k0 = pl.multiple_of(t * tk, tk)
        cms = []
        for mp in range(2):
            s = _dot_nt(kx_ref[mp, pl.ds(k0, tk), :], qm_ref[mp, qs, :])
            s_ref[mp, :, qs] = s
            cms.append(jnp.max(s, axis=0, keepdims=True))
        return tuple(cms)

    def values(t, p_ref, a, qs=full):
        k0 = pl.multiple_of(t * tk, tk)
        vt = vt_ref[:, pl.ds(k0, tk)]
        for mp in range(2):
            acc_ref[mp, :, qs] = a[mp] * acc_ref[mp, :, qs] + jnp.dot(
                vt, p_ref[mp, :, qs], preferred_element_type=F32)

    def softmax(s_cur, p_cur, m, cm, mask=None, qs=full):
        m_new, a_new = [], []
        for mp in range(2):
            s = s_cur[mp, :, qs]
            if mask is None:
                c = cm[mp]
            else:
                s = s + mask[:, qs]
                c = jnp.max(s, axis=0, keepdims=True)
            mn = jnp.maximum(m[mp], c)
            a_new.append(jnp.exp2(m[mp] - mn))
            p_cur[mp, :, qs] = jnp.exp2(s - mn).astype(BF16)
            m_new.append(mn)
        return tuple(m_new), tuple(a_new)

    two = lambda x: (x, x)
    never = jnp.full((1, tk), NEG, F32)

    def query_tile(i, _):
        rows = pl.ds(pl.multiple_of(i * tq, tq), tq)
        last = 2 * i + 1
        q = q_ref[0, 0, 0, rows, :].astype(F32) * (d ** -0.5 * LOG2E)
        qm_ref[0] = jnp.where(lane < d, q, qfeat).astype(BF16)
        qm_ref[1] = jnp.where(lane >= d, q, qfeat).astype(BF16)
        acc_ref[...] = jnp.zeros_like(acc_ref)
        pb_ref[...] = jnp.zeros_like(pb_ref)

        def step(t, s_cur, s_nxt, p_cur, p_prev, carry):
            m, a, cm = carry
            m, a_new = softmax(s_cur, p_cur, m, cm)
            values(jnp.where(t == 0, last, t - 1), p_prev, a)
            return m, a_new, scores(t + 1, s_nxt)

        def pair(pp, carry):
            carry = step(2 * pp, sa_ref, sb_ref, pa_ref, pb_ref, carry)
            return step(2 * pp + 1, sb_ref, sa_ref, pb_ref, pa_ref, carry)

        cm = scores(0, sa_ref)
        scores(last, sb_ref, upper)
        m_up, _ = softmax(sb_ref, pb_ref, two(never), None, mask=mask_ref.at[1], qs=upper)
        m = tuple(jnp.concatenate([never, x], axis=1) for x in m_up)
        carry = (m, two(jnp.ones((1, tq), F32)), cm)
        m, a, cm = lax.fori_loop(0, i, pair, carry)
        m, a_d0 = softmax(sa_ref, pa_ref, m, cm, mask=mask_ref.at[0])
        values(jnp.where(i == 0, last, 2 * i - 1), pb_ref, a)
        values(2 * i, pa_ref, a_d0)

        o1 = acc_ref[0, :2 * d] * (1.0 / acc_ref[0, 2 * d:2 * d + 1])
        o2 = acc_ref[1, :2 * d] * (lam / acc_ref[1, 2 * d:2 * d + 1])
        o = o1 - o2
        o = o * lax.rsqrt(jnp.mean(o * o, axis=0, keepdims=True) + RMS_EPS)
        o_ref[0, 0, rows, :] = (o.T * g_ref[...] * (1.0 - lambda_init)).astype(o_ref.dtype)
        return 0

    lax.fori_loop(0, S // tq, query_tile, 0)


def _diff_attention(qkv, lq1, lk1, lq2, lk2, subln_g, lambda_init, *, tk=256):
    B, _, _, S, _ = qkv.shape
    H, d = DA_HEADS, DA_HEAD_DIM
    tk = min(tk, S // 2)
    tq = 2 * tk
    assert S % tq == 0 and S <= 1 << (2 * POS_BITS)
    slopes = 2.0 ** (-8.0 * jnp.arange(1, H + 1, dtype=F32) / H)
    pos = jnp.arange(S, dtype=jnp.int32)[:, None]
    slot = jnp.arange(2 * d, dtype=jnp.int32)[None, :] % d
    feat = jnp.where(slot < 3, pos >> POS_BITS,
                     jnp.where(slot < 6, pos & ((1 << POS_BITS) - 1), 0)).astype(BF16)
    vec = lambda a: a.reshape(1, -1).astype(F32)
    small = lambda n: pl.BlockSpec((1, n), lambda b, h: (0, 0))
    scores = pltpu.VMEM((2, tk, tq), F32)
    probs = pltpu.VMEM((2, tk, tq), BF16)
    return pl.pallas_call(
        functools.partial(_diff_body, tk=tk, lambda_init=lambda_init),
        grid=(B, H),
        in_specs=[pl.BlockSpec(memory_space=pltpu.SMEM),
                  pl.BlockSpec((S, 2 * d), lambda b, h: (0, 0)),
                  pl.BlockSpec((1, 1, 1, S, 2 * d), lambda b, h: (b, h, 0, 0, 0)),
                  pl.BlockSpec((1, 1, 1, S, 2 * d), lambda b, h: (b, H + h, 0, 0, 0)),
                  pl.BlockSpec((1, 1, 1, S, 2 * d), lambda b, h: (b, 2 * H + h, 0, 0, 0)),
                  small(d), small(d), small(d), small(d), small(2 * d)],
        out_specs=pl.BlockSpec((1, 1, S, 2 * d), lambda b, h: (b, h, 0, 0)),
        out_shape=jax.ShapeDtypeStruct((B, H, S, 2 * d), BF16),
        scratch_shapes=[pltpu.VMEM((2 * d + SUM_ROWS, S), BF16),
                        pltpu.VMEM((2, S, 2 * d), BF16),
                        pltpu.VMEM((2, tk, tq), F32),
                        pltpu.VMEM((2, tq, 2 * d), BF16),
                        scores, scores, probs, probs,
                        pltpu.VMEM((2, 2 * d + SUM_ROWS, tq), F32)],
        compiler_params=_params("parallel", "parallel"),
        name="diff_attn",
    )(slopes, feat, qkv, qkv, qkv, vec(lq1), vec(lk1), vec(lq2), vec(lk2), vec(subln_g))


def _dil_body(slopes_ref, *refs, tt, dils):
    G = len(dils)
    in_refs, o_ref = refs[:3 * G], refs[3 * G]
    og_refs = refs[3 * G + 1:3 * G + 1 + G]
    lse_refs = refs[3 * G + 1 + G:3 * G + 1 + 2 * G]
    bias_ref = refs[3 * G + 1 + 2 * G]
    h = pl.program_id(1)
    t = pl.program_id(2)
    slope = slopes_ref[h] * LOG2E
    blk = DIL_BLOCK
    scale = DIL_HEAD_DIM ** -0.5 * LOG2E
    base = (lax.broadcasted_iota(jnp.int32, (blk, 2 * blk), 0)
            - lax.broadcasted_iota(jnp.int32, (blk, 2 * blk), 1))

    for g in range(G):
        window = DIL_GROUPS[g][0] // dils[g]
        for first in range(2):
            dist = base + (0 if first else blk)
            bias_ref[2 * g + first] = jnp.where((dist >= 0) & (dist <= window),
                                                -(slope * dils[g]) * dist.astype(F32), NEG)

    for g in range(G):
        dil = dils[g]
        q_ref, k_ref, v_ref = in_refs[3 * g:3 * g + 3]
        rows = tt // dil
        nb = rows // blk

        def block(idx, _, dil=dil, q_ref=q_ref, k_ref=k_ref, v_ref=v_ref, rows=rows, nb=nb, g=g):
            r = idx // nb
            bi = idx % nb
            lq = pl.multiple_of(bi * blk, blk)
            l0 = t * rows + lq
            ks = pl.multiple_of(jnp.maximum(l0 - blk, 0), blk)
            qb = q_ref[0, 0, r, pl.ds(lq, blk), :]
            kc = k_ref[0, 0, r, pl.ds(ks, 2 * blk), :]
            vc = v_ref[0, 0, r, pl.ds(ks, 2 * blk), :]
            s = _dot_nt(qb, kc) * scale + bias_ref[2 * g + (l0 == 0).astype(jnp.int32)]
            m = jnp.max(s, axis=-1, keepdims=True)
            p = jnp.exp2(s - m)
            l = jnp.sum(p, axis=-1, keepdims=True)
            acc = jnp.dot(p.astype(BF16), vc, preferred_element_type=F32)
            tok = pl.ds(lq * dil + r, blk, stride=dil)
            og_refs[g][tok, :] = acc * (1.0 / l)
            lse_refs[g][tok, :] = jnp.broadcast_to(m + jnp.log2(l), (blk, LANES))
            return 0

        lax.fori_loop(0, dil * nb, block, 0, unroll=DIL_UNROLL)

    ch = 256
    for c in range(tt // ch):
        sl = slice(c * ch, (c + 1) * ch)
        lses = [lse_refs[g][sl, :] for g in range(G)]
        top = functools.reduce(jnp.maximum, lses)
        ws = [jnp.exp2(lse - top) for lse in lses]
        num = sum(w * og_refs[g][sl, :] for g, w in enumerate(ws))
        o_ref[0, 0, sl, :] = (num / sum(ws)).astype(o_ref.dtype)


def _dilated_attention(qkvs, *, tt=2048):
    dils = tuple(d for _, d in DIL_GROUPS)
    B = qkvs[0].shape[0]
    S = qkvs[0].shape[2] * qkvs[0].shape[3]
    H, dh = DIL_HEADS, DIL_HEAD_DIM
    tt = min(tt, S)
    assert all(tt % (d * DIL_BLOCK) == 0 and S // d >= 2 * DIL_BLOCK for d in dils)
    slopes = 2.0 ** (-8.0 * jnp.arange(1, H + 1, dtype=F32) / H)
    in_specs = [pl.BlockSpec(memory_space=pltpu.SMEM)]
    args = [slopes]
    for a, dil in zip(qkvs, dils):
        L = S // dil
        in_specs += [
            pl.BlockSpec((1, 1, dil, tt // dil, dh), lambda b, h, t: (b, h, 0, t, 0)),
            pl.BlockSpec((1, 1, dil, L, dh), lambda b, h, t: (b, H + h, 0, 0, 0)),
            pl.BlockSpec((1, 1, dil, L, dh), lambda b, h, t: (b, 2 * H + h, 0, 0, 0))]
        args += [a, a, a]
    return pl.pallas_call(
        functools.partial(_dil_body, tt=tt, dils=dils),
        grid=(B, H, S // tt),
        in_specs=in_specs,
        out_specs=pl.BlockSpec((1, 1, tt, dh), lambda b, h, t: (b, h, t, 0)),
        out_shape=jax.ShapeDtypeStruct((B, H, S, dh), BF16),
        scratch_shapes=([pltpu.VMEM((tt, LANES), F32)] * (2 * len(dils))
                        + [pltpu.VMEM((2 * len(dils), DIL_BLOCK, 2 * DIL_BLOCK), F32)]),
        compiler_params=_params("parallel", "parallel", "arbitrary"),
        name="dilated_attn",
    )(*args)


def _gate_body(x_ref, wg_ref, w2_ref, b_ref, o_ref):
    C = GLA_CHUNK
    g_low = jnp.dot(x_ref[0].astype(BF16), wg_ref[...], preferred_element_type=F32)
    z = jnp.dot(g_low.astype(BF16), w2_ref[...], preferred_element_type=F32) + b_ref[...]
    log_a = (jnp.minimum(z, 0.0) - jnp.log1p(jnp.exp(-jnp.abs(z)))) / GLA_TAU
    tri = (lax.broadcasted_iota(jnp.int32, (C, C), 0)
           >= lax.broadcasted_iota(jnp.int32, (C, C), 1)).astype(BF16)
    for c in range(log_a.shape[0] // C):
        a = log_a[c * C:(c + 1) * C]
        hi = a.astype(BF16)
        rem = a - hi.astype(F32)
        mid = rem.astype(BF16)
        lo = (rem - mid.astype(F32)).astype(BF16)
        cum = (jnp.dot(tri, hi, preferred_element_type=F32)
               + jnp.dot(tri, mid, preferred_element_type=F32)
               + jnp.dot(tri, lo, preferred_element_type=F32))
        o_ref[0, c * C:(c + 1) * C, :] = cum


def _gla_gate(x, wg, w2, b, *, tm=512):
    B, S, D = x.shape
    tm = min(tm, S)
    return pl.pallas_call(
        _gate_body,
        grid=(B, S // tm),
        in_specs=[pl.BlockSpec((1, tm, D), lambda b_, i: (b_, i, 0)),
                  pl.BlockSpec((D, LANES), lambda b_, i: (0, 0)),
                  pl.BlockSpec((LANES, GLA_DK), lambda b_, i: (0, 0)),
                  pl.BlockSpec((1, GLA_DK), lambda b_, i: (0, 0))],
        out_specs=pl.BlockSpec((1, tm, GLA_DK), lambda b_, i: (b_, i, 0)),
        out_shape=jax.ShapeDtypeStruct((B, S, GLA_DK), F32),
        compiler_params=_params("parallel", "parallel"),
        name="gla_gate",
    )(x, wg, w2, b.reshape(1, GLA_DK))


def _gla_body(q_ref, k_ref, v_ref, r_ref, b_ref, g_ref, o_ref, st_ref, *, hp):
    C = GLA_CHUNK
    dk, dv = GLA_DK // GLA_HEADS, GLA_DV // GLA_HEADS

    @pl.when(pl.program_id(2) == 0)
    def _():
        st_ref[...] = jnp.zeros_like(st_ref)

    causal = (lax.broadcasted_iota(jnp.int32, (C, C), 0)
              >= lax.broadcasted_iota(jnp.int32, (C, C), 1))
    st = [st_ref[hh] for hh in range(hp)]
    for c in range(q_ref.shape[1] // C):
        sl = slice(c * C, (c + 1) * C)
        for hh in range(hp):
            ck = slice(hh * dk, (hh + 1) * dk)
            cv = slice(hh * dv, (hh + 1) * dv)
            bc = b_ref[0, sl, ck]
            bl = bc[C - 1:C, :]
            kk = k_ref[0, sl, ck]
            q_dec = (q_ref[0, sl, ck] * (dk ** -0.5) * jnp.exp(bc)).astype(BF16)
            k_intra = (kk * jnp.exp(-bc)).astype(BF16)
            k_state = (kk * jnp.exp(bl - bc)).astype(BF16)
            vb = v_ref[0, sl, cv].astype(BF16)
            s = jnp.where(causal, _dot_nt(q_dec, k_intra), 0.0)
            o = (jnp.dot(s.astype(BF16), vb, preferred_element_type=F32)
                 + _dot_nt(q_dec, st[hh].astype(BF16)))
            st[hh] = st[hh] * jnp.exp(bl) + lax.dot_general(
                vb, k_state, (((0,), (0,)), ((), ())), preferred_element_type=F32)
            o = o * lax.rsqrt(jnp.mean(o * o, axis=-1, keepdims=True) + RMS_EPS) * g_ref[...]
            r = r_ref[0, sl, cv]
            o_ref[0, hh, sl, :] = (o * (r * jax.nn.sigmoid(r))).astype(o_ref.dtype)
    for hh in range(hp):
        st_ref[hh] = st[hh]


def _gla(proj, bcum, gnorm_g, *, ts=512, hp=4):
    B, S, _ = proj.shape
    H = GLA_HEADS
    dk, dv = GLA_DK // H, GLA_DV // H
    ts = min(ts, S)
    wk, wv = hp * dk, hp * dv
    nk = GLA_DK // wk
    nv = 2 * GLA_DK // wv
    return pl.pallas_call(
        functools.partial(_gla_body, hp=hp),
        grid=(B, H // hp, S // ts),
        in_specs=[pl.BlockSpec((1, ts, wk), lambda b, h, t: (b, t, h)),
                  pl.BlockSpec((1, ts, wk), lambda b, h, t: (b, t, nk + h)),
                  pl.BlockSpec((1, ts, wv), lambda b, h, t: (b, t, nv + h)),
                  pl.BlockSpec((1, ts, wv), lambda b, h, t: (b, t, nv + H // hp + h)),
                  pl.BlockSpec((1, ts, wk), lambda b, h, t: (b, t, h)),
                  pl.BlockSpec((1, dv), lambda b, h, t: (0, 0))],
        out_specs=pl.BlockSpec((1, hp, ts, dv), lambda b, h, t: (b, h, t, 0)),
        out_shape=jax.ShapeDtypeStruct((B, H, S, dv), BF16),
        scratch_shapes=[pltpu.VMEM((hp, dv, dk), F32)],
        compiler_params=_params("parallel", "parallel", "arbitrary"),
        name="gla",
    )(proj, proj, proj, proj, bcum, gnorm_g.reshape(1, dv))


def _diff_mixer(x, w_in, lq1, lk1, lq2, lk2, subln_g, layer_idx):
    lambda_init = 0.8 - 0.6 * math.exp(-0.3 * layer_idx)
    qkv = _proj(x, w_in.astype(BF16))
    return _diff_attention(qkv, lq1, lk1, lq2, lk2, subln_g, lambda_init)


def _dil_mixer(x, w_in):
    wb = w_in.astype(BF16)
    per_group = 3 * DIL_HEADS * DIL_HEAD_DIM
    qkvs = [_proj(x, wb[:, g * per_group:(g + 1) * per_group], dil=dil)
            for g, (_, dil) in enumerate(DIL_GROUPS)]
    return _dilated_attention(qkvs)


def _gla_mixer(x, w_in, w_gate2, b_gate, gnorm_g):
    n_main = 2 * GLA_DK + 2 * GLA_DV
    wb = w_in.astype(BF16)
    proj = _proj(x, wb[:, :n_main], head_major=False, out_dtype=F32)
    wg = jnp.pad(wb[:, n_main:], ((0, 0), (0, LANES - GLA_GATE_RANK)))
    w2 = jnp.pad(w_gate2.astype(BF16), ((0, LANES - GLA_GATE_RANK), (0, 0)))
    bcum = _gla_gate(x, wg, w2, b_gate)
    return _gla(proj, bcum, gnorm_g)


def _finish_layer(o, x, w_out, g1, b1, w1, w2, g2, b2):
    return _out_mlp(o, w_out.astype(BF16), x, g1, b1, w1.astype(BF16), w2.astype(BF16), g2, b2)


def kernel(x, l0_w_in, l0_lam_q1, l0_lam_k1, l0_lam_q2, l0_lam_k2, l0_subln_g, l0_w_out, l0_ln1_g, l0_ln1_b, l0_w_ff1, l0_w_ff2, l0_ln2_g, l0_ln2_b, l1_w_in, l1_w_out, l1_ln1_g, l1_ln1_b, l1_w_ff1, l1_w_ff2, l1_ln2_g, l1_ln2_b, l2_w_in, l2_w_gate2, l2_b_gate, l2_gnorm_g, l2_w_out, l2_ln1_g, l2_ln1_b, l2_w_ff1, l2_w_ff2, l2_ln2_g, l2_ln2_b, l3_w_in, l3_lam_q1, l3_lam_k1, l3_lam_q2, l3_lam_k2, l3_subln_g, l3_w_out, l3_ln1_g, l3_ln1_b, l3_w_ff1, l3_w_ff2, l3_ln2_g, l3_ln2_b):
    o = _diff_mixer(x, l0_w_in, l0_lam_q1, l0_lam_k1, l0_lam_q2, l0_lam_k2, l0_subln_g, 0)
    x = _finish_layer(o, x, l0_w_out, l0_ln1_g, l0_ln1_b, l0_w_ff1, l0_w_ff2, l0_ln2_g, l0_ln2_b)
    o = _dil_mixer(x, l1_w_in)
    x = _finish_layer(o, x, l1_w_out, l1_ln1_g, l1_ln1_b, l1_w_ff1, l1_w_ff2, l1_ln2_g, l1_ln2_b)
    o = _gla_mixer(x, l2_w_in, l2_w_gate2, l2_b_gate, l2_gnorm_g)
    x = _finish_layer(o, x, l2_w_out, l2_ln1_g, l2_ln1_b, l2_w_ff1, l2_w_ff2, l2_ln2_g, l2_ln2_b)
    o = _diff_mixer(x, l3_w_in, l3_lam_q1, l3_lam_k1, l3_lam_q2, l3_lam_k2, l3_subln_g, 3)
    x = _finish_layer(o, x, l3_w_out, l3_ln1_g, l3_ln1_b, l3_w_ff1, l3_w_ff2, l3_ln2_g, l3_ln2_b)
    return x
```

```python
import functools
import math

import jax
import jax.numpy as jnp
from jax import lax
from jax.experimental import pallas as pl
from jax.experimental.pallas import tpu as pltpu

F32 = jnp.float32
BF16 = jnp.bfloat16

D_MODEL = 1024
DEPTH = 4
N_MIXERS = 3
DA_HEADS = 8
DA_HEAD_DIM = 64
DIL_GROUPS = ((128, 1), (512, 4), (2048, 16))
DIL_HEADS = 8
DIL_HEAD_DIM = 128
DIL_BLOCK = 128
GLA_HEADS = 4
GLA_DK = D_MODEL // 2
GLA_DV = D_MODEL
GLA_GATE_RANK = 16
GLA_TAU = 16.0
GLA_CHUNK = 64
D_FF = 4 * D_MODEL
ALPHA = (2 * DEPTH) ** 0.25
LN_EPS = 1e-5
RMS_EPS = 1e-6

LANES = 128
VMEM_LIMIT = 56 * 1024 * 1024
NEG = -1e30
LOG2E = math.log2(math.e)
SUM_ROWS = 16
SPLIT_STRIDE = 4
LN_ROWS = 512
POS_BITS = 6


def _params(*sem):
    return pltpu.CompilerParams(dimension_semantics=sem, vmem_limit_bytes=VMEM_LIMIT)


def _layer_norm(z, g, b):
    mu = jnp.mean(z, axis=-1, keepdims=True)
    zc = z - mu
    var = jnp.mean(zc * zc, axis=-1, keepdims=True)
    return zc * lax.rsqrt(var + LN_EPS) * g + b


def _dot_nt(a, b):
    return lax.dot_general(a, b, (((1,), (1,)), ((), ())), preferred_element_type=F32)


def _proj_body(x_ref, w_ref, o_ref, xb_ref, *acc_scratch, dil, head_major):
    @pl.when(pl.program_id(2) == 0)
    def _():
        xb_ref[...] = x_ref[0].astype(BF16)

    acc = jnp.dot(xb_ref[...], w_ref[...], preferred_element_type=F32)
    tm, tn = acc.shape
    if not head_major:
        o_ref[0] = acc.astype(o_ref.dtype)
    elif dil == 1:
        for c in range(tn // LANES):
            o_ref[0, c, 0] = acc[:, c * LANES:(c + 1) * LANES].astype(o_ref.dtype)
    else:
        acc_ref = acc_scratch[0]
        rows = tm // dil
        for c in range(tn // LANES):
            acc_ref[c] = acc[:, c * LANES:(c + 1) * LANES]
            if dil <= SPLIT_STRIDE:
                for r in range(dil):
                    o_ref[0, c, r] = acc_ref[c, pl.ds(r, rows, stride=dil), :].astype(o_ref.dtype)
            else:
                tmp_ref = acc_scratch[1]
                d2 = dil // SPLIT_STRIDE
                for r1 in range(SPLIT_STRIDE):
                    tmp_ref[r1] = acc_ref[c, pl.ds(r1, tm // SPLIT_STRIDE, stride=SPLIT_STRIDE), :]
                    for r2 in range(d2):
                        o_ref[0, c, r2 * SPLIT_STRIDE + r1] = tmp_ref[
                            r1, pl.ds(r2, rows, stride=d2), :].astype(o_ref.dtype)


def _proj(x, w, *, dil=1, head_major=True, out_dtype=BF16, tm=1024, tn=1024):
    B, S, D = x.shape
    N = w.shape[1]
    tm = min(tm, S)
    tn = min(tn, N)
    assert S % tm == 0 and N % tn == 0 and tm % dil == 0 and tn % LANES == 0
    if head_major:
        out_shape = jax.ShapeDtypeStruct((B, N // LANES, dil, S // dil, LANES), out_dtype)
        out_spec = pl.BlockSpec((1, tn // LANES, dil, tm // dil, LANES),
                                lambda b, i, j: (b, j, 0, i, 0))
    else:
        out_shape = jax.ShapeDtypeStruct((B, S, N), out_dtype)
        out_spec = pl.BlockSpec((1, tm, tn), lambda b, i, j: (b, i, j))
    scratch = [pltpu.VMEM((tm, D), BF16)]
    if head_major and dil > 1:
        scratch.append(pltpu.VMEM((tn // LANES, tm, LANES), F32))
        if dil > SPLIT_STRIDE:
            scratch.append(pltpu.VMEM((SPLIT_STRIDE, tm // SPLIT_STRIDE, LANES), F32))
    return pl.pallas_call(
        functools.partial(_proj_body, dil=dil, head_major=head_major),
        grid=(B, S // tm, N // tn),
        in_specs=[pl.BlockSpec((1, tm, D), lambda b, i, j: (b, i, 0)),
                  pl.BlockSpec((D, tn), lambda b, i, j: (0, j))],
        out_specs=out_spec,
        out_shape=out_shape,
        scratch_shapes=scratch,
        compiler_params=_params("parallel", "parallel", "arbitrary"),
        name="proj_in",
    )(x, w)


def _out_mlp_body(o_ref, wo_ref, x_ref, g1_ref, b1_ref, w1_ref, w2_ref, g2_ref, b2_ref, y_ref,
                  x1_ref, xb_ref, acc_ref, *, nblk):
    k = pl.program_id(2)
    last = pl.num_programs(2) - 1
    tm = x1_ref.shape[0]
    ch = min(LN_ROWS, tm)
    chunks = [slice(c * ch, (c + 1) * ch) for c in range(tm // ch)]

    def hidden(xb):
        h = jnp.dot(xb, w1_ref[...], preferred_element_type=F32)
        h = jnp.square(jnp.maximum(h, 0.0)).astype(BF16)
        return jnp.dot(h, w2_ref[...], preferred_element_type=F32)

    @pl.when(k == 0)
    def _():
        for rows in chunks:
            o = jnp.concatenate([o_ref[0, n, rows, :] for n in range(nblk)], axis=-1)
            y = jnp.dot(o, wo_ref[...], preferred_element_type=F32)
            x1 = _layer_norm(ALPHA * x_ref[0, rows, :] + y, g1_ref[...], b1_ref[...])
            x1_ref[rows, :] = x1
            xb = x1.astype(BF16)
            xb_ref[rows, :] = xb
            acc_ref[rows, :] = hidden(xb)

    @pl.when((k > 0) & (k < last))
    def _():
        acc_ref[...] += hidden(xb_ref[...])

    @pl.when(k == last)
    def _():
        for rows in chunks:
            z = ALPHA * x1_ref[rows, :] + (acc_ref[rows, :] + hidden(xb_ref[rows, :]))
            y_ref[0, rows, :] = _layer_norm(z, g2_ref[...], b2_ref[...])


def _out_mlp(o, wo, x, g1, b1, w1, w2, g2, b2, *, tm=1024, tf=1024):
    B, nblk, S, wblk = o.shape
    D = x.shape[-1]
    FF = w1.shape[1]
    tm = min(tm, S)
    assert FF // tf >= 2
    row = lambda a: a.reshape(1, D)
    const = lambda shape: pl.BlockSpec(shape, lambda b, i, k: (0, 0))
    return pl.pallas_call(
        functools.partial(_out_mlp_body, nblk=nblk),
        grid=(B, S // tm, FF // tf),
        in_specs=[pl.BlockSpec((1, nblk, tm, wblk), lambda b, i, k: (b, 0, i, 0)),
                  const((nblk * wblk, D)),
                  pl.BlockSpec((1, tm, D), lambda b, i, k: (b, i, 0)),
                  const((1, D)), const((1, D)),
                  pl.BlockSpec((D, tf), lambda b, i, k: (0, k)),
                  pl.BlockSpec((tf, D), lambda b, i, k: (k, 0)),
                  const((1, D)), const((1, D))],
        out_specs=pl.BlockSpec((1, tm, D), lambda b, i, k: (b, i, 0)),
        out_shape=jax.ShapeDtypeStruct(x.shape, F32),
        scratch_shapes=[pltpu.VMEM((tm, D), F32), pltpu.VMEM((tm, D), BF16),
                        pltpu.VMEM((tm, D), F32)],
        compiler_params=_params("parallel", "parallel", "arbitrary"),
        name="out_mlp_ln",
    )(o, wo, x, row(g1), row(b1), w1, w2, row(g2), row(b2))


def _diff_body(slopes_ref, feat_ref, q_ref, k_ref, v_ref, lq1_ref, lk1_ref, lq2_ref, lk2_ref, g_ref,
               o_ref, vt_ref, kx_ref, mask_ref, qm_ref, sa_ref, sb_ref, pa_ref, pb_ref, acc_ref,
               *, tk, lambda_init):
    h = pl.program_id(1)
    d = DA_HEAD_DIM
    tq = 2 * tk
    S = k_ref.shape[3]
    lane = lax.broadcasted_iota(jnp.int32, (1, 2 * d), 1)
    digit = lane & (d - 1)
    full = slice(None)
    upper = slice(tk, tq)

    ch = min(512, S)
    for c in range(S // ch):
        sl = slice(c * ch, (c + 1) * ch)
        vt_ref[:2 * d, sl] = v_ref[0, 0, 0, sl, :].astype(F32).T.astype(BF16)
        feat = feat_ref[sl, :]
        k = k_ref[0, 0, 0, sl, :]
        kx_ref[0, sl, :] = jnp.where(lane < d, k, feat)
        kx_ref[1, sl, :] = jnp.where(lane >= d, k, feat)
    vt_ref[2 * d:, :] = jnp.ones((SUM_ROWS, S), BF16)
    rel = (lax.broadcasted_iota(jnp.int32, (tk, tq), 0)
           - lax.broadcasted_iota(jnp.int32, (tk, tq), 1))
    for c in range(2):
        mask_ref[c] = jnp.where(rel <= -c * tk, 0.0, NEG)

    sv = jnp.full((1, 2 * d), slopes_ref[h] * LOG2E, F32)
    s_a = sv.astype(BF16).astype(F32)
    s_b = (sv - s_a).astype(BF16).astype(F32)
    s_c = (sv - s_a - s_b).astype(BF16).astype(F32)
    piece = jnp.where((digit == 0) | (digit == 3), s_a,
                      jnp.where((digit == 1) | (digit == 4), s_b, s_c))
    qfeat = jnp.where(digit < 3, piece * (1 << POS_BITS), jnp.where(digit < 6, piece, 0.0))
    lam = (jnp.exp(jnp.sum(lq1_ref[...] * lk1_ref[...], axis=-1, keepdims=True))
           - jnp.exp(jnp.sum(lq2_ref[...] * lk2_ref[...], axis=-1, keepdims=True))
           + lambda_init)

    def scores(t, s_ref, qs=full):
        k0 = pl.multiple_of(t * tk, tk)
        cms = []
        for mp in range(2):
            s = _dot_nt(kx_ref[mp, pl.ds(k0, tk), :], qm_ref[mp, qs, :])
            s_ref[mp, :, qs] = s
            cms.append(jnp.max(s, axis=0, keepdims=True))
        return tuple(cms)

    def values(t, p_ref, a, qs=full):
        k0 = pl.multiple_of(t * tk, tk)
        vt = vt_ref[:, pl.ds(k0, tk)]
        for mp in range(2):
            acc_ref[mp, :, qs] = a[mp] * acc_ref[mp, :, qs] + jnp.dot(
                vt, p_ref[mp, :, qs], preferred_element_type=F32)

    def softmax(s_cur, p_cur, m, cm, mask=None, qs=full):
        m_new, a_new = [], []
        for mp in range(2):
            s = s_cur[mp, :, qs]
            if mask is None:
                c = cm[mp]
            else:
                s = s + mask[:, qs]
                c = jnp.max(s, axis=0, keepdims=True)
            mn = jnp.maximum(m[mp], c)
            a_new.append(jnp.exp2(m[mp] - mn))
            p_cur[mp, :, qs] = jnp.exp2(s - mn).astype(BF16)
            m_new.append(mn)
        return tuple(m_new), tuple(a_new)

    two = lambda x: (x, x)
    never = jnp.full((1, tk), NEG, F32)

    def query_tile(i, _):
        rows = pl.ds(pl.multiple_of(i * tq, tq), tq)
        last = 2 * i + 1
        q = q_ref[0, 0, 0, rows, :].astype(F32) * (d ** -0.5 * LOG2E)
        qm_ref[0] = jnp.where(lane < d, q, qfeat).astype(BF16)
        qm_ref[1] = jnp.where(lane >= d, q, qfeat).astype(BF16)
        acc_ref[...] = jnp.zeros_like(acc_ref)
        pb_ref[...] = jnp.zeros_like(pb_ref)

        def step(t, s_cur, s_nxt, p_cur, p_prev, carry):
            m, a, cm = carry
            m, a_new = softmax(s_cur, p_cur, m, cm)
            values(jnp.where(t == 0, last, t - 1), p_prev, a)
            return m, a_new, scores(t + 1, s_nxt)

        def pair(pp, carry):
            carry = step(2 * pp, sa_ref, sb_ref, pa_ref, pb_ref, carry)
            return step(2 * pp + 1, sb_ref, sa_ref, pb_ref, pa_ref, carry)

        cm = scores(0, sa_ref)
        scores(last, sb_ref, upper)
        m_up, _ = softmax(sb_ref, pb_ref, two(never), None, mask=mask_ref.at[1], qs=upper)
        m = tuple(jnp.concatenate([never, x], axis=1) for x in m_up)
        carry = (m, two(jnp.ones((1, tq), F32)), cm)
        m, a, cm = lax.fori_loop(0, i, pair, carry)
        m, a_d0 = softmax(sa_ref, pa_ref, m, cm, mask=mask_ref.at[0])
        values(jnp.where(i == 0, last, 2 * i - 1), pb_ref, a)
        values(2 * i, pa_ref, a_d0)

        o1 = acc_ref[0, :2 * d] * (1.0 / acc_ref[0, 2 * d:2 * d + 1])
        o2 = acc_ref[1, :2 * d] * (lam / acc_ref[1, 2 * d:2 * d + 1])
        o = o1 - o2
        o = o * lax.rsqrt(jnp.mean(o * o, axis=0, keepdims=True) + RMS_EPS)
        o_ref[0, 0, rows, :] = (o.T * g_ref[...] * (1.0 - lambda_init)).astype(o_ref.dtype)
        return 0

    lax.fori_loop(0, S // tq, query_tile, 0)


def _diff_attention(qkv, lq1, lk1, lq2, lk2, subln_g, lambda_init, *, tk=256):
    B, _, _, S, _ = qkv.shape
    H, d = DA_HEADS, DA_HEAD_DIM
    tk = min(tk, S // 2)
    tq = 2 * tk
    assert S % tq == 0 and S <= 1 << (2 * POS_BITS)
    slopes = 2.0 ** (-8.0 * jnp.arange(1, H + 1, dtype=F32) / H)
    pos = jnp.arange(S, dtype=jnp.int32)[:, None]
    slot = jnp.arange(2 * d, dtype=jnp.int32)[None, :] % d
    feat = jnp.where(slot < 3, pos >> POS_BITS,
                     jnp.where(slot < 6, pos & ((1 << POS_BITS) - 1), 0)).astype(BF16)
    vec = lambda a: a.reshape(1, -1).astype(F32)
    small = lambda n: pl.BlockSpec((1, n), lambda b, h: (0, 0))
    scores = pltpu.VMEM((2, tk, tq), F32)
    probs = pltpu.VMEM((2, tk, tq), BF16)
    return pl.pallas_call(
        functools.partial(_diff_body, tk=tk, lambda_init=lambda_init),
        grid=(B, H),
        in_specs=[pl.BlockSpec(memory_space=pltpu.SMEM),
                  pl.BlockSpec((S, 2 * d), lambda b, h: (0, 0)),
                  pl.BlockSpec((1, 1, 1, S, 2 * d), lambda b, h: (b, h, 0, 0, 0)),
                  pl.BlockSpec((1, 1, 1, S, 2 * d), lambda b, h: (b, H + h, 0, 0, 0)),
                  pl.BlockSpec((1, 1, 1, S, 2 * d), lambda b, h: (b, 2 * H + h, 0, 0, 0)),
                  small(d), small(d), small(d), small(d), small(2 * d)],
        out_specs=pl.BlockSpec((1, 1, S, 2 * d), lambda b, h: (b, h, 0, 0)),
        out_shape=jax.ShapeDtypeStruct((B, H, S, 2 * d), BF16),
        scratch_shapes=[pltpu.VMEM((2 * d + SUM_ROWS, S), BF16),
                        pltpu.VMEM((2, S, 2 * d), BF16),
                        pltpu.VMEM((2, tk, tq), F32),
                        pltpu.VMEM((2, tq, 2 * d), BF16),
                        scores, scores, probs, probs,
                        pltpu.VMEM((2, 2 * d + SUM_ROWS, tq), F32)],
        compiler_params=_params("parallel", "parallel"),
        name="diff_attn",
    )(slopes, feat, qkv, qkv, qkv, vec(lq1), vec(lk1), vec(lq2), vec(lk2), vec(subln_g))


def _dil_body(slopes_ref, *refs, tt, dils):
    G = len(dils)
    in_refs, o_ref = refs[:3 * G], refs[3 * G]
    og_refs = refs[3 * G + 1:3 * G + 1 + G]
    lse_refs = refs[3 * G + 1 + G:3 * G + 1 + 2 * G]
    bias_ref, tmp_ref = refs[3 * G + 1 + 2 * G:]
    h = pl.program_id(1)
    t = pl.program_id(2)
    slope = slopes_ref[h] * LOG2E
    blk = DIL_BLOCK
    scale = DIL_HEAD_DIM ** -0.5 * LOG2E
    base = (lax.broadcasted_iota(jnp.int32, (blk, 2 * blk), 0)
            - lax.broadcasted_iota(jnp.int32, (blk, 2 * blk), 1))

    for g in range(G):
        window = DIL_GROUPS[g][0] // dils[g]
        for first in range(2):
            dist = base + (0 if first else blk)
            bias_ref[2 * g + first] = jnp.where((dist >= 0) & (dist <= window),
                                                -(slope * dils[g]) * dist.astype(F32), NEG)

    for g in range(G):
        dil = dils[g]
        q_ref, k_ref, v_ref = in_refs[3 * g:3 * g + 3]
        rows = tt // dil
        nb = rows // blk

        def block(r, bi, dil=dil, q_ref=q_ref, k_ref=k_ref, v_ref=v_ref, rows=rows, g=g):
            lq = bi * blk
            l0 = t * rows + lq
            ks = pl.multiple_of(jnp.maximum(l0 - blk, 0), blk)
            qb = q_ref[0, 0, r, lq:lq + blk, :]
            kc = k_ref[0, 0, r, pl.ds(ks, 2 * blk), :]
            vc = v_ref[0, 0, r, pl.ds(ks, 2 * blk), :]
            s = _dot_nt(qb, kc) * scale + bias_ref[2 * g + (l0 == 0).astype(jnp.int32)]
            m = jnp.max(s, axis=-1, keepdims=True)
            p = jnp.exp2(s - m)
            l = jnp.sum(p, axis=-1, keepdims=True)
            acc = jnp.dot(p.astype(BF16), vc, preferred_element_type=F32)
            return acc * (1.0 / l), jnp.broadcast_to(m + jnp.log2(l), (blk, LANES))

        if dil <= SPLIT_STRIDE:
            for r in range(dil):
                for bi in range(nb):
                    tok = pl.ds(bi * blk * dil + r, blk, stride=dil)
                    og_refs[g][tok, :], lse_refs[g][tok, :] = block(r, bi)
        else:
            d2 = dil // SPLIT_STRIDE
            for r1 in range(SPLIT_STRIDE):
                for r2 in range(d2):
                    for bi in range(nb):
                        mid = pl.ds(bi * blk * d2 + r2, blk, stride=d2)
                        tmp_ref[0, r1, mid, :], tmp_ref[1, r1, mid, :] = block(
                            r2 * SPLIT_STRIDE + r1, bi)
                tok = pl.ds(r1, tt // SPLIT_STRIDE, stride=SPLIT_STRIDE)
                og_refs[g][tok, :] = tmp_ref[0, r1]
                lse_refs[g][tok, :] = tmp_ref[1, r1]

    ch = 256
    for c in range(tt // ch):
        sl = slice(c * ch, (c + 1) * ch)
        lses = [lse_refs[g][sl, :] for g in range(G)]
        top = functools.reduce(jnp.maximum, lses)
        ws = [jnp.exp2(lse - top) for lse in lses]
        num = sum(w * og_refs[g][sl, :] for g, w in enumerate(ws))
        o_ref[0, 0, sl, :] = (num / sum(ws)).astype(o_ref.dtype)


def _dilated_attention(qkvs, *, tt=2048):
    dils = tuple(d for _, d in DIL_GROUPS)
    B = qkvs[0].shape[0]
    S = qkvs[0].shape[2] * qkvs[0].shape[3]
    H, dh = DIL_HEADS, DIL_HEAD_DIM
    tt = min(tt, S)
    assert all(tt % (d * DIL_BLOCK) == 0 and S // d >= 2 * DIL_BLOCK for d in dils)
    slopes = 2.0 ** (-8.0 * jnp.arange(1, H + 1, dtype=F32) / H)
    in_specs = [pl.BlockSpec(memory_space=pltpu.SMEM)]
    args = [slopes]
    for a, dil in zip(qkvs, dils):
        L = S // dil
        in_specs += [
            pl.BlockSpec((1, 1, dil, tt // dil, dh), lambda b, h, t: (b, h, 0, t, 0)),
            pl.BlockSpec((1, 1, dil, L, dh), lambda b, h, t: (b, H + h, 0, 0, 0)),
            pl.BlockSpec((1, 1, dil, L, dh), lambda b, h, t: (b, 2 * H + h, 0, 0, 0))]
        args += [a, a, a]
    return pl.pallas_call(
        functools.partial(_dil_body, tt=tt, dils=dils),
        grid=(B, H, S // tt),
        in_specs=in_specs,
        out_specs=pl.BlockSpec((1, 1, tt, dh), lambda b, h, t: (b, h, t, 0)),
        out_shape=jax.ShapeDtypeStruct((B, H, S, dh), BF16),
        scratch_shapes=([pltpu.VMEM((tt, LANES), F32)] * (2 * len(dils))
                        + [pltpu.VMEM((2 * len(dils), DIL_BLOCK, 2 * DIL_BLOCK), F32),
                           pltpu.VMEM((2, SPLIT_STRIDE, tt // SPLIT_STRIDE, LANES), F32)]),
        compiler_params=_params("parallel", "parallel", "arbitrary"),
        name="dilated_attn",
    )(*args)


def _gate_body(x_ref, wg_ref, w2_ref, b_ref, o_ref):
    C = GLA_CHUNK
    g_low = jnp.dot(x_ref[0].astype(BF16), wg_ref[...], preferred_element_type=F32)
    z = jnp.dot(g_low.astype(BF16), w2_ref[...], preferred_element_type=F32) + b_ref[...]
    log_a = (jnp.minimum(z, 0.0) - jnp.log1p(jnp.exp(-jnp.abs(z)))) / GLA_TAU
    tri = (lax.broadcasted_iota(jnp.int32, (C, C), 0)
           >= lax.broadcasted_iota(jnp.int32, (C, C), 1)).astype(BF16)
    for c in range(log_a.shape[0] // C):
        a = log_a[c * C:(c + 1) * C]
        hi = a.astype(BF16)
        rem = a - hi.astype(F32)
        mid = rem.astype(BF16)
        lo = (rem - mid.astype(F32)).astype(BF16)
        cum = (jnp.dot(tri, hi, preferred_element_type=F32)
               + jnp.dot(tri, mid, preferred_element_type=F32)
               + jnp.dot(tri, lo, preferred_element_type=F32))
        o_ref[0, c * C:(c + 1) * C, :] = cum


def _gla_gate(x, wg, w2, b, *, tm=512):
    B, S, D = x.shape
    tm = min(tm, S)
    return pl.pallas_call(
        _gate_body,
        grid=(B, S // tm),
        in_specs=[pl.BlockSpec((1, tm, D), lambda b_, i: (b_, i, 0)),
                  pl.BlockSpec((D, LANES), lambda b_, i: (0, 0)),
                  pl.BlockSpec((LANES, GLA_DK), lambda b_, i: (0, 0)),
                  pl.BlockSpec((1, GLA_DK), lambda b_, i: (0, 0))],
        out_specs=pl.BlockSpec((1, tm, GLA_DK), lambda b_, i: (b_, i, 0)),
        out_shape=jax.ShapeDtypeStruct((B, S, GLA_DK), F32),
        compiler_params=_params("parallel", "parallel"),
        name="gla_gate",
    )(x, wg, w2, b.reshape(1, GLA_DK))


def _gla_body(q_ref, k_ref, v_ref, r_ref, b_ref, g_ref, o_ref, st_ref, *, hp):
    C = GLA_CHUNK
    dk, dv = GLA_DK // GLA_HEADS, GLA_DV // GLA_HEADS

    @pl.when(pl.program_id(2) == 0)
    def _():
        st_ref[...] = jnp.zeros_like(st_ref)

    causal = (lax.broadcasted_iota(jnp.int32, (C, C), 0)
              >= lax.broadcasted_iota(jnp.int32, (C, C), 1))
    st = [st_ref[hh] for hh in range(hp)]
    for c in range(q_ref.shape[1] // C):
        sl = slice(c * C, (c + 1) * C)
        for hh in range(hp):
            ck = slice(hh * dk, (hh + 1) * dk)
            cv = slice(hh * dv, (hh + 1) * dv)
            bc = b_ref[0, sl, ck]
            bl = bc[C - 1:C, :]
            kk = k_ref[0, sl, ck]
            q_dec = (q_ref[0, sl, ck] * (dk ** -0.5) * jnp.exp(bc)).astype(BF16)
            k_intra = (kk * jnp.exp(-bc)).astype(BF16)
            k_state = (kk * jnp.exp(bl - bc)).astype(BF16)
            vb = v_ref[0, sl, cv].astype(BF16)
            s = jnp.where(causal, _dot_nt(q_dec, k_intra), 0.0)
            o = (jnp.dot(s.astype(BF16), vb, preferred_element_type=F32)
                 + _dot_nt(q_dec, st[hh].astype(BF16)))
            st[hh] = st[hh] * jnp.exp(bl) + lax.dot_general(
                vb, k_state, (((0,), (0,)), ((), ())), preferred_element_type=F32)
            o = o * lax.rsqrt(jnp.mean(o * o, axis=-1, keepdims=True) + RMS_EPS) * g_ref[...]
            r = r_ref[0, sl, cv]
            o_ref[0, hh, sl, :] = (o * (r * jax.nn.sigmoid(r))).astype(o_ref.dtype)
    for hh in range(hp):
        st_ref[hh] = st[hh]


def _gla(proj, bcum, gnorm_g, *, ts=512, hp=4):
    B, S, _ = proj.shape
    H = GLA_HEADS
    dk, dv = GLA_DK // H, GLA_DV // H
    ts = min(ts, S)
    wk, wv = hp * dk, hp * dv
    nk = GLA_DK // wk
    nv = 2 * GLA_DK // wv
    return pl.pallas_call(
        functools.partial(_gla_body, hp=hp),
        grid=(B, H // hp, S // ts),
        in_specs=[pl.BlockSpec((1, ts, wk), lambda b, h, t: (b, t, h)),
                  pl.BlockSpec((1, ts, wk), lambda b, h, t: (b, t, nk + h)),
                  pl.BlockSpec((1, ts, wv), lambda b, h, t: (b, t, nv + h)),
                  pl.BlockSpec((1, ts, wv), lambda b, h, t: (b, t, nv + H // hp + h)),
                  pl.BlockSpec((1, ts, wk), lambda b, h, t: (b, t, h)),
                  pl.BlockSpec((1, dv), lambda b, h, t: (0, 0))],
        out_specs=pl.BlockSpec((1, hp, ts, dv), lambda b, h, t: (b, h, t, 0)),
        out_shape=jax.ShapeDtypeStruct((B, H, S, dv), BF16),
        scratch_shapes=[pltpu.VMEM((hp, dv, dk), F32)],
        compiler_params=_params("parallel", "parallel", "arbitrary"),
        name="gla",
    )(proj, proj, proj, proj, bcum, gnorm_g.reshape(1, dv))


def _diff_mixer(x, w_in, lq1, lk1, lq2, lk2, subln_g, layer_idx):
    lambda_init = 0.8 - 0.6 * math.exp(-0.3 * layer_idx)
    qkv = _proj(x, w_in.astype(BF16), tn=3072)
    return _diff_attention(qkv, lq1, lk1, lq2, lk2, subln_g, lambda_init)


def _dil_mixer(x, w_in):
    wb = w_in.astype(BF16)
    per_group = 3 * DIL_HEADS * DIL_HEAD_DIM
    qkvs = [_proj(x, wb[:, g * per_group:(g + 1) * per_group], dil=dil)
            for g, (_, dil) in enumerate(DIL_GROUPS)]
    return _dilated_attention(qkvs)


def _gla_mixer(x, w_in, w_gate2, b_gate, gnorm_g):
    n_main = 2 * GLA_DK + 2 * GLA_DV
    wb = w_in.astype(BF16)
    proj = _proj(x, wb[:, :n_main], head_major=False, out_dtype=F32, tn=1536)
    wg = jnp.pad(wb[:, n_main:], ((0, 0), (0, LANES - GLA_GATE_RANK)))
    w2 = jnp.pad(w_gate2.astype(BF16), ((0, LANES - GLA_GATE_RANK), (0, 0)))
    bcum = _gla_gate(x, wg, w2, b_gate)
    return _gla(proj, bcum, gnorm_g)


def _finish_layer(o, x, w_out, g1, b1, w1, w2, g2, b2):
    return _out_mlp(o, w_out.astype(BF16), x, g1, b1, w1.astype(BF16), w2.astype(BF16), g2, b2)


def kernel(x, l0_w_in, l0_lam_q1, l0_lam_k1, l0_lam_q2, l0_lam_k2, l0_subln_g, l0_w_out, l0_ln1_g, l0_ln1_b, l0_w_ff1, l0_w_ff2, l0_ln2_g, l0_ln2_b, l1_w_in, l1_w_out, l1_ln1_g, l1_ln1_b, l1_w_ff1, l1_w_ff2, l1_ln2_g, l1_ln2_b, l2_w_in, l2_w_gate2, l2_b_gate, l2_gnorm_g, l2_w_out, l2_ln1_g, l2_ln1_b, l2_w_ff1, l2_w_ff2, l2_ln2_g, l2_ln2_b, l3_w_in, l3_lam_q1, l3_lam_k1, l3_lam_q2, l3_lam_k2, l3_subln_g, l3_w_out, l3_ln1_g, l3_ln1_b, l3_w_ff1, l3_w_ff2, l3_ln2_g, l3_ln2_b):
    o = _diff_mixer(x, l0_w_in, l0_lam_q1, l0_lam_k1, l0_lam_q2, l0_lam_k2, l0_subln_g, 0)
    x = _finish_layer(o, x, l0_w_out, l0_ln1_g, l0_ln1_b, l0_w_ff1, l0_w_ff2, l0_ln2_g, l0_ln2_b)
    o = _dil_mixer(x, l1_w_in)
    x = _finish_layer(o, x, l1_w_out, l1_ln1_g, l1_ln1_b, l1_w_ff1, l1_w_ff2, l1_ln2_g, l1_ln2_b)
    o = _gla_mixer(x, l2_w_in, l2_w_gate2, l2_b_gate, l2_gnorm_g)
    x = _finish_layer(o, x, l2_w_out, l2_ln1_g, l2_ln1_b, l2_w_ff1, l2_w_ff2, l2_ln2_g, l2_ln2_b)
    o = _diff_mixer(x, l3_w_in, l3_lam_q1, l3_lam_k1, l3_lam_q2, l3_lam_k2, l3_subln_g, 3)
    x = _finish_layer(o, x, l3_w_out, l3_ln1_g, l3_ln1_b, l3_w_ff1, l3_w_ff2, l3_ln2_g, l3_ln2_b)
    return x
```

```python
import functools
import math

import jax
import jax.numpy as jnp
from jax import lax
from jax.experimental import pallas as pl
from jax.experimental.pallas import tpu as pltpu

F32 = jnp.float32
BF16 = jnp.bfloat16

D_MODEL = 1024
DEPTH = 4
N_MIXERS = 3
DA_HEADS = 8
DA_HEAD_DIM = 64
DIL_GROUPS = ((128, 1), (512, 4), (2048, 16))
DIL_HEADS = 8
DIL_HEAD_DIM = 128
DIL_BLOCK = 128
GLA_HEADS = 4
GLA_DK = D_MODEL // 2
GLA_DV = D_MODEL
GLA_GATE_RANK = 16
GLA_TAU = 16.0
GLA_CHUNK = 64
D_FF = 4 * D_MODEL
ALPHA = (2 * DEPTH) ** 0.25
LN_EPS = 1e-5
RMS_EPS = 1e-6

LANES = 128
VMEM_LIMIT = 56 * 1024 * 1024
NEG = -1e30
LOG2E = math.log2(math.e)
SUM_ROWS = 16
SPLIT_STRIDE = 4
LN_ROWS = 512
POS_BITS = 6


def _params(*sem):
    return pltpu.CompilerParams(dimension_semantics=sem, vmem_limit_bytes=VMEM_LIMIT)


def _layer_norm(z, g, b):
    mu = jnp.mean(z, axis=-1, keepdims=True)
    zc = z - mu
    var = jnp.mean(zc * zc, axis=-1, keepdims=True)
    return zc * lax.rsqrt(var + LN_EPS) * g + b


def _dot_nt(a, b):
    return lax.dot_general(a, b, (((1,), (1,)), ((), ())), preferred_element_type=F32)


def _proj_body(x_ref, w_ref, o_ref, xb_ref, *acc_scratch, dil, head_major):
    @pl.when(pl.program_id(2) == 0)
    def _():
        xb_ref[...] = x_ref[0].astype(BF16)

    acc = jnp.dot(xb_ref[...], w_ref[...], preferred_element_type=F32)
    tm, tn = acc.shape
    if not head_major:
        o_ref[0] = acc.astype(o_ref.dtype)
    elif dil == 1:
        for c in range(tn // LANES):
            o_ref[0, c, 0] = acc[:, c * LANES:(c + 1) * LANES].astype(o_ref.dtype)
    else:
        acc_ref = acc_scratch[0]
        rows = tm // dil
        for c in range(tn // LANES):
            acc_ref[c] = acc[:, c * LANES:(c + 1) * LANES]
            if dil <= SPLIT_STRIDE:
                for r in range(dil):
                    o_ref[0, c, r] = acc_ref[c, pl.ds(r, rows, stride=dil), :].astype(o_ref.dtype)
            else:
                tmp_ref = acc_scratch[1]
                d2 = dil // SPLIT_STRIDE
                for r1 in range(SPLIT_STRIDE):
                    tmp_ref[r1] = acc_ref[c, pl.ds(r1, tm // SPLIT_STRIDE, stride=SPLIT_STRIDE), :]
                    for r2 in range(d2):
                        o_ref[0, c, r2 * SPLIT_STRIDE + r1] = tmp_ref[
                            r1, pl.ds(r2, rows, stride=d2), :].astype(o_ref.dtype)


def _proj(x, w, *, dil=1, head_major=True, out_dtype=BF16, tm=1024, tn=1024):
    B, S, D = x.shape
    N = w.shape[1]
    tm = min(tm, S)
    tn = min(tn, N)
    assert S % tm == 0 and N % tn == 0 and tm % dil == 0 and tn % LANES == 0
    if head_major:
        out_shape = jax.ShapeDtypeStruct((B, N // LANES, dil, S // dil, LANES), out_dtype)
        out_spec = pl.BlockSpec((1, tn // LANES, dil, tm // dil, LANES),
                                lambda b, i, j: (b, j, 0, i, 0))
    else:
        out_shape = jax.ShapeDtypeStruct((B, S, N), out_dtype)
        out_spec = pl.BlockSpec((1, tm, tn), lambda b, i, j: (b, i, j))
    scratch = [pltpu.VMEM((tm, D), BF16)]
    if head_major and dil > 1:
        scratch.append(pltpu.VMEM((tn // LANES, tm, LANES), F32))
        if dil > SPLIT_STRIDE:
            scratch.append(pltpu.VMEM((SPLIT_STRIDE, tm // SPLIT_STRIDE, LANES), F32))
    return pl.pallas_call(
        functools.partial(_proj_body, dil=dil, head_major=head_major),
        grid=(B, S // tm, N // tn),
        in_specs=[pl.BlockSpec((1, tm, D), lambda b, i, j: (b, i, 0)),
                  pl.BlockSpec((D, tn), lambda b, i, j: (0, j))],
        out_specs=out_spec,
        out_shape=out_shape,
        scratch_shapes=scratch,
        compiler_params=_params("parallel", "parallel", "arbitrary"),
        name="proj_in",
    )(x, w)


def _out_mlp_body(o_ref, wo_ref, x_ref, g1_ref, b1_ref, w1_ref, w2_ref, g2_ref, b2_ref, y_ref,
                  x1_ref, xb_ref, acc_ref, *, nblk):
    k = pl.program_id(2)
    last = pl.num_programs(2) - 1
    tm = x1_ref.shape[0]
    ch = min(LN_ROWS, tm)
    chunks = [slice(c * ch, (c + 1) * ch) for c in range(tm // ch)]

    def hidden(xb):
        h = jnp.dot(xb, w1_ref[...], preferred_element_type=F32)
        h = jnp.square(jnp.maximum(h, 0.0)).astype(BF16)
        return jnp.dot(h, w2_ref[...], preferred_element_type=F32)

    @pl.when(k == 0)
    def _():
        for rows in chunks:
            o = jnp.concatenate([o_ref[0, n, rows, :] for n in range(nblk)], axis=-1)
            y = jnp.dot(o, wo_ref[...], preferred_element_type=F32)
            x1 = _layer_norm(ALPHA * x_ref[0, rows, :] + y, g1_ref[...], b1_ref[...])
            x1_ref[rows, :] = x1
            xb = x1.astype(BF16)
            xb_ref[rows, :] = xb
            acc_ref[rows, :] = hidden(xb)

    @pl.when((k > 0) & (k < last))
    def _():
        acc_ref[...] += hidden(xb_ref[...])

    @pl.when(k == last)
    def _():
        for rows in chunks:
            z = ALPHA * x1_ref[rows, :] + (acc_ref[rows, :] + hidden(xb_ref[rows, :]))
            y_ref[0, rows, :] = _layer_norm(z, g2_ref[...], b2_ref[...])


def _out_mlp(o, wo, x, g1, b1, w1, w2, g2, b2, *, tm=1024, tf=1024):
    B, nblk, S, wblk = o.shape
    D = x.shape[-1]
    FF = w1.shape[1]
    tm = min(tm, S)
    assert FF // tf >= 2
    row = lambda a: a.reshape(1, D)
    const = lambda shape: pl.BlockSpec(shape, lambda b, i, k: (0, 0))
    return pl.pallas_call(
        functools.partial(_out_mlp_body, nblk=nblk),
        grid=(B, S // tm, FF // tf),
        in_specs=[pl.BlockSpec((1, nblk, tm, wblk), lambda b, i, k: (b, 0, i, 0)),
                  const((nblk * wblk, D)),
                  pl.BlockSpec((1, tm, D), lambda b, i, k: (b, i, 0)),
                  const((1, D)), const((1, D)),
                  pl.BlockSpec((D, tf), lambda b, i, k: (0, k)),
                  pl.BlockSpec((tf, D), lambda b, i, k: (k, 0)),
                  const((1, D)), const((1, D))],
        out_specs=pl.BlockSpec((1, tm, D), lambda b, i, k: (b, i, 0)),
        out_shape=jax.ShapeDtypeStruct(x.shape, F32),
        scratch_shapes=[pltpu.VMEM((tm, D), F32), pltpu.VMEM((tm, D), BF16),
                        pltpu.VMEM((tm, D), F32)],
        compiler_params=_params("parallel", "parallel", "arbitrary"),
        name="out_mlp_ln",
    )(o, wo, x, row(g1), row(b1), w1, w2, row(g2), row(b2))


def _diff_body(slopes_ref, feat_ref, q_ref, k_ref, v_ref, lq1_ref, lk1_ref, lq2_ref, lk2_ref, g_ref,
               o_ref, vt_ref, kx_ref, mask_ref, qm_ref, sa_ref, sb_ref, pa_ref, pb_ref, acc_ref,
               *, tk, lambda_init):
    h = pl.program_id(1)
    d = DA_HEAD_DIM
    tq = 2 * tk
    S = k_ref.shape[3]
    lane = lax.broadcasted_iota(jnp.int32, (1, 2 * d), 1)
    digit = lane & (d - 1)
    full = slice(None)
    upper = slice(tk, tq)

    ch = min(512, S)
    for c in range(S // ch):
        sl = slice(c * ch, (c + 1) * ch)
        vt_ref[:2 * d, sl] = v_ref[0, 0, 0, sl, :].astype(F32).T.astype(BF16)
        feat = feat_ref[sl, :]
        k = k_ref[0, 0, 0, sl, :]
        kx_ref[0, sl, :] = jnp.where(lane < d, k, feat)
        kx_ref[1, sl, :] = jnp.where(lane >= d, k, feat)
    vt_ref[2 * d:, :] = jnp.ones((SUM_ROWS, S), BF16)
    rel = (lax.broadcasted_iota(jnp.int32, (tk, tq), 0)
           - lax.broadcasted_iota(jnp.int32, (tk, tq), 1))
    for c in range(2):
        mask_ref[c] = jnp.where(rel <= -c * tk, 0.0, NEG)

    sv = jnp.full((1, 2 * d), slopes_ref[h] * LOG2E, F32)
    s_a = sv.astype(BF16).astype(F32)
    s_b = (sv - s_a).astype(BF16).astype(F32)
    s_c = (sv - s_a - s_b).astype(BF16).astype(F32)
    piece = jnp.where((digit == 0) | (digit == 3), s_a,
                      jnp.where((digit == 1) | (digit == 4), s_b, s_c))
    qfeat = jnp.where(digit < 3, piece * (1 << POS_BITS), jnp.where(digit < 6, piece, 0.0))
    lam = (jnp.exp(jnp.sum(lq1_ref[...] * lk1_ref[...], axis=-1, keepdims=True))
           - jnp.exp(jnp.sum(lq2_ref[...] * lk2_ref[...], axis=-1, keepdims=True))
           + lambda_init)

    def scores(t, s_ref, qs=full):
        k0 = pl.multiple_of(t * tk, tk)
        cms = []
        for mp in range(2):
            s = _dot_nt(kx_ref[mp, pl.ds(k0, tk), :], qm_ref[mp, qs, :])
            s_ref[mp, :, qs] = s
            cms.append(jnp.max(s, axis=0, keepdims=True))
        return tuple(cms)

    def values(t, p_ref, a, qs=full):
        k0 = pl.multiple_of(t * tk, tk)
        vt = vt_ref[:, pl.ds(k0, tk)]
        for mp in range(2):
            acc_ref[mp, :, qs] = a[mp] * acc_ref[mp, :, qs] + jnp.dot(
                vt, p_ref[mp, :, qs], preferred_element_type=F32)

    def softmax(s_cur, p_cur, m, cm, mask=None, qs=full):
        m_new, a_new = [], []
        for mp in range(2):
            s = s_cur[mp, :, qs]
            if mask is None:
                c = cm[mp]
            else:
                s = s + mask[:, qs]
                c = jnp.max(s, axis=0, keepdims=True)
            mn = jnp.maximum(m[mp], c)
            a_new.append(jnp.exp2(m[mp] - mn))
            p_cur[mp, :, qs] = jnp.exp2(s - mn).astype(BF16)
            m_new.append(mn)
        return tuple(m_new), tuple(a_new)

    two = lambda x: (x, x)
    never = jnp.full((1, tk), NEG, F32)

    def query_tile(i, _):
        rows = pl.ds(pl.multiple_of(i * tq, tq), tq)
        last = 2 * i + 1
        q = q_ref[0, 0, 0, rows, :].astype(F32) * (d ** -0.5 * LOG2E)
        qm_ref[0] = jnp.where(lane < d, q, qfeat).astype(BF16)
        qm_ref[1] = jnp.where(lane >= d, q, qfeat).astype(BF16)
        acc_ref[...] = jnp.zeros_like(acc_ref)
        pb_ref[...] = jnp.zeros_like(pb_ref)

        def step(t, s_cur, s_nxt, p_cur, p_prev, carry):
            m, a, cm = carry
            m, a_new = softmax(s_cur, p_cur, m, cm)
            values(jnp.where(t == 0, last, t - 1), p_prev, a)
            return m, a_new, scores(t + 1, s_nxt)

        def pair(pp, carry):
            carry = step(2 * pp, sa_ref, sb_ref, pa_ref, pb_ref, carry)
            return step(2 * pp + 1, sb_ref, sa_ref, pb_ref, pa_ref, carry)

        cm = scores(0, sa_ref)
        scores(last, sb_ref, upper)
        m_up, _ = softmax(sb_ref, pb_ref, two(never), None, mask=mask_ref.at[1], qs=upper)
        m = tuple(jnp.concatenate([never, x], axis=1) for x in m_up)
        carry = (m, two(jnp.ones((1, tq), F32)), cm)
        m, a, cm = lax.fori_loop(0, i, pair, carry)
        m, a_d0 = softmax(sa_ref, pa_ref, m, cm, mask=mask_ref.at[0])
        values(jnp.where(i == 0, last, 2 * i - 1), pb_ref, a)
        values(2 * i, pa_ref, a_d0)

        o1 = acc_ref[0, :2 * d] * (1.0 / acc_ref[0, 2 * d:2 * d + 1])
        o2 = acc_ref[1, :2 * d] * (lam / acc_ref[1, 2 * d:2 * d + 1])
        o = o1 - o2
        o = o * lax.rsqrt(jnp.mean(o * o, axis=0, keepdims=True) + RMS_EPS)
        o_ref[0, 0, rows, :] = (o.T * g_ref[...] * (1.0 - lambda_init)).astype(o_ref.dtype)
        return 0

    lax.fori_loop(0, S // tq, query_tile, 0)


def _diff_attention(qkv, lq1, lk1, lq2, lk2, subln_g, lambda_init, *, tk=256):
    B, _, _, S, _ = qkv.shape
    H, d = DA_HEADS, DA_HEAD_DIM
    tk = min(tk, S // 2)
    tq = 2 * tk
    assert S % tq == 0 and S <= 1 << (2 * POS_BITS)
    slopes = 2.0 ** (-8.0 * jnp.arange(1, H + 1, dtype=F32) / H)
    pos = jnp.arange(S, dtype=jnp.int32)[:, None]
    slot = jnp.arange(2 * d, dtype=jnp.int32)[None, :] % d
    feat = jnp.where(slot < 3, pos >> POS_BITS,
                     jnp.where(slot < 6, pos & ((1 << POS_BITS) - 1), 0)).astype(BF16)
    vec = lambda a: a.reshape(1, -1).astype(F32)
    small = lambda n: pl.BlockSpec((1, n), lambda b, h: (0, 0))
    scores = pltpu.VMEM((2, tk, tq), F32)
    probs = pltpu.VMEM((2, tk, tq), BF16)
    return pl.pallas_call(
        functools.partial(_diff_body, tk=tk, lambda_init=lambda_init),
        grid=(B, H),
        in_specs=[pl.BlockSpec(memory_space=pltpu.SMEM),
                  pl.BlockSpec((S, 2 * d), lambda b, h: (0, 0)),
                  pl.BlockSpec((1, 1, 1, S, 2 * d), lambda b, h: (b, h, 0, 0, 0)),
                  pl.BlockSpec((1, 1, 1, S, 2 * d), lambda b, h: (b, H + h, 0, 0, 0)),
                  pl.BlockSpec((1, 1, 1, S, 2 * d), lambda b, h: (b, 2 * H + h, 0, 0, 0)),
                  small(d), small(d), small(d), small(d), small(2 * d)],
        out_specs=pl.BlockSpec((1, 1, S, 2 * d), lambda b, h: (b, h, 0, 0)),
        out_shape=jax.ShapeDtypeStruct((B, H, S, 2 * d), BF16),
        scratch_shapes=[pltpu.VMEM((2 * d + SUM_ROWS, S), BF16),
                        pltpu.VMEM((2, S, 2 * d), BF16),
                        pltpu.VMEM((2, tk, tq), F32),
                        pltpu.VMEM((2, tq, 2 * d), BF16),
                        scores, scores, probs, probs,
                        pltpu.VMEM((2, 2 * d + SUM_ROWS, tq), F32)],
        compiler_params=_params("parallel", "parallel"),
        name="diff_attn",
    )(slopes, feat, qkv, qkv, qkv, vec(lq1), vec(lk1), vec(lq2), vec(lk2), vec(subln_g))


def _dil_body(slopes_ref, *refs, tt, dils):
    G = len(dils)
    in_refs, o_ref = refs[:3 * G], refs[3 * G]
    og_refs = refs[3 * G + 1:3 * G + 1 + G]
    lse_refs = refs[3 * G + 1 + G:3 * G + 1 + 2 * G]
    bias_ref, tmp_ref = refs[3 * G + 1 + 2 * G:]
    h = pl.program_id(1)
    t = pl.program_id(2)
    slope = slopes_ref[h] * LOG2E
    blk = DIL_BLOCK
    scale = DIL_HEAD_DIM ** -0.5 * LOG2E
    base = (lax.broadcasted_iota(jnp.int32, (blk, 2 * blk), 0)
            - lax.broadcasted_iota(jnp.int32, (blk, 2 * blk), 1))

    for g in range(G):
        window = DIL_GROUPS[g][0] // dils[g]
        for first in range(2):
            dist = base + (0 if first else blk)
            bias_ref[2 * g + first] = jnp.where((dist >= 0) & (dist <= window),
                                                -(slope * dils[g]) * dist.astype(F32), NEG)

    for g in range(G):
        dil = dils[g]
        q_ref, k_ref, v_ref = in_refs[3 * g:3 * g + 3]
        rows = tt // dil
        nb = rows // blk

        def block(r, bi, dil=dil, q_ref=q_ref, k_ref=k_ref, v_ref=v_ref, rows=rows, g=g):
            lq = bi * blk
            l0 = t * rows + lq
            ks = pl.multiple_of(jnp.maximum(l0 - blk, 0), blk)
            qb = q_ref[0, 0, r, lq:lq + blk, :]
            kc = k_ref[0, 0, r, pl.ds(ks, 2 * blk), :]
            vc = v_ref[0, 0, r, pl.ds(ks, 2 * blk), :]
            s = _dot_nt(qb, kc) * scale + bias_ref[2 * g + (l0 == 0).astype(jnp.int32)]
            m = jnp.max(s, axis=-1, keepdims=True)
            p = jnp.exp2(s - m)
            l = jnp.sum(p, axis=-1, keepdims=True)
            acc = jnp.dot(p.astype(BF16), vc, preferred_element_type=F32)
            return acc * (1.0 / l), jnp.broadcast_to(m + jnp.log2(l), (blk, LANES))

        if dil <= SPLIT_STRIDE:
            for r in range(dil):
                for bi in range(nb):
                    tok = pl.ds(bi * blk * dil + r, blk, stride=dil)
                    og_refs[g][tok, :], lse_refs[g][tok, :] = block(r, bi)
        else:
            d2 = dil // SPLIT_STRIDE
            for r1 in range(SPLIT_STRIDE):
                for r2 in range(d2):
                    for bi in range(nb):
                        mid = pl.ds(bi * blk * d2 + r2, blk, stride=d2)
                        tmp_ref[0, r1, mid, :], tmp_ref[1, r1, mid, :] = block(
                            r2 * SPLIT_STRIDE + r1, bi)
                tok = pl.ds(r1, tt // SPLIT_STRIDE, stride=SPLIT_STRIDE)
                og_refs[g][tok, :] = tmp_ref[0, r1]
                lse_refs[g][tok, :] = tmp_ref[1, r1]

    ch = 256
    for c in range(tt // ch):
        sl = slice(c * ch, (c + 1) * ch)
        lses = [lse_refs[g][sl, :] for g in range(G)]
        top = functools.reduce(jnp.maximum, lses)
        ws = [jnp.exp2(lse - top) for lse in lses]
        num = sum(w * og_refs[g][sl, :] for g, w in enumerate(ws))
        o_ref[0, 0, sl, :] = (num / sum(ws)).astype(o_ref.dtype)


def _dilated_attention(qkvs, *, tt=2048):
    dils = tuple(d for _, d in DIL_GROUPS)
    B = qkvs[0].shape[0]
    S = qkvs[0].shape[2] * qkvs[0].shape[3]
    H, dh = DIL_HEADS, DIL_HEAD_DIM
    tt = min(tt, S)
    assert all(tt % (d * DIL_BLOCK) == 0 and S // d >= 2 * DIL_BLOCK for d in dils)
    slopes = 2.0 ** (-8.0 * jnp.arange(1, H + 1, dtype=F32) / H)
    in_specs = [pl.BlockSpec(memory_space=pltpu.SMEM)]
    args = [slopes]
    for a, dil in zip(qkvs, dils):
        L = S // dil
        in_specs += [
            pl.BlockSpec((1, 1, dil, tt // dil, dh), lambda b, h, t: (b, h, 0, t, 0)),
            pl.BlockSpec((1, 1, dil, L, dh), lambda b, h, t: (b, H + h, 0, 0, 0)),
            pl.BlockSpec((1, 1, dil, L, dh), lambda b, h, t: (b, 2 * H + h, 0, 0, 0))]
        args += [a, a, a]
    return pl.pallas_call(
        functools.partial(_dil_body, tt=tt, dils=dils),
        grid=(B, H, S // tt),
        in_specs=in_specs,
        out_specs=pl.BlockSpec((1, 1, tt, dh), lambda b, h, t: (b, h, t, 0)),
        out_shape=jax.ShapeDtypeStruct((B, H, S, dh), BF16),
        scratch_shapes=([pltpu.VMEM((tt, LANES), F32)] * (2 * len(dils))
                        + [pltpu.VMEM((2 * len(dils), DIL_BLOCK, 2 * DIL_BLOCK), F32),
                           pltpu.VMEM((2, SPLIT_STRIDE, tt // SPLIT_STRIDE, LANES), F32)]),
        compiler_params=_params("parallel", "parallel", "arbitrary"),
        name="dilated_attn",
    )(*args)


def _gate_body(x_ref, wg_ref, w2_ref, b_ref, o_ref):
    C = GLA_CHUNK
    g_low = jnp.dot(x_ref[0].astype(BF16), wg_ref[...], preferred_element_type=F32)
    z = jnp.dot(g_low.astype(BF16), w2_ref[...], preferred_element_type=F32) + b_ref[...]
    log_a = (jnp.minimum(z, 0.0) - jnp.log1p(jnp.exp(-jnp.abs(z)))) / GLA_TAU
    tri = (lax.broadcasted_iota(jnp.int32, (C, C), 0)
           >= lax.broadcasted_iota(jnp.int32, (C, C), 1)).astype(BF16)
    for c in range(log_a.shape[0] // C):
        a = log_a[c * C:(c + 1) * C]
        hi = a.astype(BF16)
        rem = a - hi.astype(F32)
        mid = rem.astype(BF16)
        lo = (rem - mid.astype(F32)).astype(BF16)
        cum = (jnp.dot(tri, hi, preferred_element_type=F32)
               + jnp.dot(tri, mid, preferred_element_type=F32)
               + jnp.dot(tri, lo, preferred_element_type=F32))
        o_ref[0, c * C:(c + 1) * C, :] = cum


def _gla_gate(x, wg, w2, b, *, tm=512):
    B, S, D = x.shape
    tm = min(tm, S)
    return pl.pallas_call(
        _gate_body,
        grid=(B, S // tm),
        in_specs=[pl.BlockSpec((1, tm, D), lambda b_, i: (b_, i, 0)),
                  pl.BlockSpec((D, LANES), lambda b_, i: (0, 0)),
                  pl.BlockSpec((LANES, GLA_DK), lambda b_, i: (0, 0)),
                  pl.BlockSpec((1, GLA_DK), lambda b_, i: (0, 0))],
        out_specs=pl.BlockSpec((1, tm, GLA_DK), lambda b_, i: (b_, i, 0)),
        out_shape=jax.ShapeDtypeStruct((B, S, GLA_DK), F32),
        compiler_params=_params("parallel", "parallel"),
        name="gla_gate",
    )(x, wg, w2, b.reshape(1, GLA_DK))


def _gla_body(q_ref, k_ref, v_ref, r_ref, b_ref, g_ref, o_ref, st_ref, *, hp):
    C = GLA_CHUNK
    dk, dv = GLA_DK // GLA_HEADS, GLA_DV // GLA_HEADS

    @pl.when(pl.program_id(2) == 0)
    def _():
        st_ref[...] = jnp.zeros_like(st_ref)

    causal = (lax.broadcasted_iota(jnp.int32, (C, C), 0)
              >= lax.broadcasted_iota(jnp.int32, (C, C), 1))
    st = [st_ref[hh] for hh in range(hp)]
    for c in range(q_ref.shape[1] // C):
        sl = slice(c * C, (c + 1) * C)
        for hh in range(hp):
            ck = slice(hh * dk, (hh + 1) * dk)
            cv = slice(hh * dv, (hh + 1) * dv)
            bc = b_ref[0, sl, ck]
            bl = bc[C - 1:C, :]
            kk = k_ref[0, sl, ck]
            q_dec = (q_ref[0, sl, ck] * (dk ** -0.5) * jnp.exp(bc)).astype(BF16)
            k_intra = (kk * jnp.exp(-bc)).astype(BF16)
            k_state = (kk * jnp.exp(bl - bc)).astype(BF16)
            vb = v_ref[0, sl, cv].astype(BF16)
            s = jnp.where(causal, _dot_nt(q_dec, k_intra), 0.0)
            o = (jnp.dot(s.astype(BF16), vb, preferred_element_type=F32)
                 + _dot_nt(q_dec, st[hh].astype(BF16)))
            st[hh] = st[hh] * jnp.exp(bl) + lax.dot_general(
                vb, k_state, (((0,), (0,)), ((), ())), preferred_element_type=F32)
            o = o * lax.rsqrt(jnp.mean(o * o, axis=-1, keepdims=True) + RMS_EPS) * g_ref[...]
            r = r_ref[0, sl, cv]
            o_ref[0, hh, sl, :] = (o * (r * jax.nn.sigmoid(r))).astype(o_ref.dtype)
    for hh in range(hp):
        st_ref[hh] = st[hh]


def _gla(proj, bcum, gnorm_g, *, ts=512, hp=4):
    B, S, _ = proj.shape
    H = GLA_HEADS
    dk, dv = GLA_DK // H, GLA_DV // H
    ts = min(ts, S)
    wk, wv = hp * dk, hp * dv
    nk = GLA_DK // wk
    nv = 2 * GLA_DK // wv
    return pl.pallas_call(
        functools.partial(_gla_body, hp=hp),
        grid=(B, H // hp, S // ts),
        in_specs=[pl.BlockSpec((1, ts, wk), lambda b, h, t: (b, t, h)),
                  pl.BlockSpec((1, ts, wk), lambda b, h, t: (b, t, nk + h)),
                  pl.BlockSpec((1, ts, wv), lambda b, h, t: (b, t, nv + h)),
                  pl.BlockSpec((1, ts, wv), lambda b, h, t: (b, t, nv + H // hp + h)),
                  pl.BlockSpec((1, ts, wk), lambda b, h, t: (b, t, h)),
                  pl.BlockSpec((1, dv), lambda b, h, t: (0, 0))],
        out_specs=pl.BlockSpec((1, hp, ts, dv), lambda b, h, t: (b, h, t, 0)),
        out_shape=jax.ShapeDtypeStruct((B, H, S, dv), BF16),
        scratch_shapes=[pltpu.VMEM((hp, dv, dk), F32)],
        compiler_params=_params("parallel", "parallel", "arbitrary"),
        name="gla",
    )(proj, proj, proj, proj, bcum, gnorm_g.reshape(1, dv))


def _diff_mixer(x, w_in, lq1, lk1, lq2, lk2, subln_g, layer_idx):
    lambda_init = 0.8 - 0.6 * math.exp(-0.3 * layer_idx)
    qkv = _proj(x, w_in.astype(BF16), tn=3072)
    return _diff_attention(qkv, lq1, lk1, lq2, lk2, subln_g, lambda_init)


def _dil_mixer(x, w_in):
    wb = w_in.astype(BF16)
    per_group = 3 * DIL_HEADS * DIL_HEAD_DIM
    qkvs = [_proj(x, wb[:, g * per_group:(g + 1) * per_group], dil=dil,
                  tn=per_group if dil == 1 else per_group // 2)
            for g, (_, dil) in enumerate(DIL_GROUPS)]
    return _dilated_attention(qkvs)


def _gla_mixer(x, w_in, w_gate2, b_gate, gnorm_g):
    n_main = 2 * GLA_DK + 2 * GLA_DV
    wb = w_in.astype(BF16)
    proj = _proj(x, wb[:, :n_main], head_major=False, out_dtype=F32, tn=1536)
    wg = jnp.pad(wb[:, n_main:], ((0, 0), (0, LANES - GLA_GATE_RANK)))
    w2 = jnp.pad(w_gate2.astype(BF16), ((0, LANES - GLA_GATE_RANK), (0, 0)))
    bcum = _gla_gate(x, wg, w2, b_gate)
    return _gla(proj, bcum, gnorm_g)


def _finish_layer(o, x, w_out, g1, b1, w1, w2, g2, b2):
    return _out_mlp(o, w_out.astype(BF16), x, g1, b1, w1.astype(BF16), w2.astype(BF16), g2, b2)


def kernel(x, l0_w_in, l0_lam_q1, l0_lam_k1, l0_lam_q2, l0_lam_k2, l0_subln_g, l0_w_out, l0_ln1_g, l0_ln1_b, l0_w_ff1, l0_w_ff2, l0_ln2_g, l0_ln2_b, l1_w_in, l1_w_out, l1_ln1_g, l1_ln1_b, l1_w_ff1, l1_w_ff2, l1_ln2_g, l1_ln2_b, l2_w_in, l2_w_gate2, l2_b_gate, l2_gnorm_g, l2_w_out, l2_ln1_g, l2_ln1_b, l2_w_ff1, l2_w_ff2, l2_ln2_g, l2_ln2_b, l3_w_in, l3_lam_q1, l3_lam_k1, l3_lam_q2, l3_lam_k2, l3_subln_g, l3_w_out, l3_ln1_g, l3_ln1_b, l3_w_ff1, l3_w_ff2, l3_ln2_g, l3_ln2_b):
    o = _diff_mixer(x, l0_w_in, l0_lam_q1, l0_lam_k1, l0_lam_q2, l0_lam_k2, l0_subln_g, 0)
    x = _finish_layer(o, x, l0_w_out, l0_ln1_g, l0_ln1_b, l0_w_ff1, l0_w_ff2, l0_ln2_g, l0_ln2_b)
    o = _dil_mixer(x, l1_w_in)
    x = _finish_layer(o, x, l1_w_out, l1_ln1_g, l1_ln1_b, l1_w_ff1, l1_w_ff2, l1_ln2_g, l1_ln2_b)
    o = _gla_mixer(x, l2_w_in, l2_w_gate2, l2_b_gate, l2_gnorm_g)
    x = _finish_layer(o, x, l2_w_out, l2_ln1_g, l2_ln1_b, l2_w_ff1, l2_w_ff2, l2_ln2_g, l2_ln2_b)
    o = _diff_mixer(x, l3_w_in, l3_lam_q1, l3_lam_k1, l3_lam_q2, l3_lam_k2, l3_subln_g, 3)
    x = _finish_layer(o, x, l3_w_out, l3_ln1_g, l3_ln1_b, l3_w_ff1, l3_w_ff2, l3_ln2_g, l3_ln2_b)
    return x
```

```python
import functools
import math

import jax
import jax.numpy as jnp
from jax import lax
from jax.experimental import pallas as pl
from jax.experimental.pallas import tpu as pltpu

F32 = jnp.float32
BF16 = jnp.bfloat16

D_MODEL = 1024
DEPTH = 4
N_MIXERS = 3
DA_HEADS = 8
DA_HEAD_DIM = 64
DIL_GROUPS = ((128, 1), (512, 4), (2048, 16))
DIL_HEADS = 8
DIL_HEAD_DIM = 128
DIL_BLOCK = 128
GLA_HEADS = 4
GLA_DK = D_MODEL // 2
GLA_DV = D_MODEL
GLA_GATE_RANK = 16
GLA_TAU = 16.0
GLA_CHUNK = 64
D_FF = 4 * D_MODEL
ALPHA = (2 * DEPTH) ** 0.25
LN_EPS = 1e-5
RMS_EPS = 1e-6

LANES = 128
VMEM_LIMIT = 56 * 1024 * 1024
NEG = -1e30
LOG2E = math.log2(math.e)
SUM_ROWS = 16
SPLIT_STRIDE = 4
LN_ROWS = 512
POS_BITS = 6


def _params(*sem):
    return pltpu.CompilerParams(dimension_semantics=sem, vmem_limit_bytes=VMEM_LIMIT)


def _layer_norm(z, g, b):
    mu = jnp.mean(z, axis=-1, keepdims=True)
    zc = z - mu
    var = jnp.mean(zc * zc, axis=-1, keepdims=True)
    return zc * lax.rsqrt(var + LN_EPS) * g + b


def _dot_nt(a, b):
    return lax.dot_general(a, b, (((1,), (1,)), ((), ())), preferred_element_type=F32)


def _proj_body(x_ref, w_ref, o_ref, xb_ref, *acc_scratch, dil, head_major):
    @pl.when(pl.program_id(2) == 0)
    def _():
        xb_ref[...] = x_ref[0].astype(BF16)

    acc = jnp.dot(xb_ref[...], w_ref[...], preferred_element_type=F32)
    tm, tn = acc.shape
    if not head_major:
        o_ref[0] = acc.astype(o_ref.dtype)
    elif dil == 1:
        for c in range(tn // LANES):
            o_ref[0, c, 0] = acc[:, c * LANES:(c + 1) * LANES].astype(o_ref.dtype)
    else:
        acc_ref = acc_scratch[0]
        rows = tm // dil
        for c in range(tn // LANES):
            acc_ref[c] = acc[:, c * LANES:(c + 1) * LANES]
            if dil <= SPLIT_STRIDE:
                for r in range(dil):
                    o_ref[0, c, r] = acc_ref[c, pl.ds(r, rows, stride=dil), :].astype(o_ref.dtype)
            else:
                tmp_ref = acc_scratch[1]
                d2 = dil // SPLIT_STRIDE
                for r1 in range(SPLIT_STRIDE):
                    tmp_ref[r1] = acc_ref[c, pl.ds(r1, tm // SPLIT_STRIDE, stride=SPLIT_STRIDE), :]
                    for r2 in range(d2):
                        o_ref[0, c, r2 * SPLIT_STRIDE + r1] = tmp_ref[
                            r1, pl.ds(r2, rows, stride=d2), :].astype(o_ref.dtype)


def _proj(x, w, *, dil=1, head_major=True, out_dtype=BF16, tm=1024, tn=1024):
    B, S, D = x.shape
    N = w.shape[1]
    tm = min(tm, S)
    tn = min(tn, N)
    assert S % tm == 0 and N % tn == 0 and tm % dil == 0 and tn % LANES == 0
    if head_major:
        out_shape = jax.ShapeDtypeStruct((B, N // LANES, dil, S // dil, LANES), out_dtype)
        out_spec = pl.BlockSpec((1, tn // LANES, dil, tm // dil, LANES),
                                lambda b, i, j: (b, j, 0, i, 0))
    else:
        out_shape = jax.ShapeDtypeStruct((B, S, N), out_dtype)
        out_spec = pl.BlockSpec((1, tm, tn), lambda b, i, j: (b, i, j))
    scratch = [pltpu.VMEM((tm, D), BF16)]
    if head_major and dil > 1:
        scratch.append(pltpu.VMEM((tn // LANES, tm, LANES), F32))
        if dil > SPLIT_STRIDE:
            scratch.append(pltpu.VMEM((SPLIT_STRIDE, tm // SPLIT_STRIDE, LANES), F32))
    return pl.pallas_call(
        functools.partial(_proj_body, dil=dil, head_major=head_major),
        grid=(B, S // tm, N // tn),
        in_specs=[pl.BlockSpec((1, tm, D), lambda b, i, j: (b, i, 0)),
                  pl.BlockSpec((D, tn), lambda b, i, j: (0, j))],
        out_specs=out_spec,
        out_shape=out_shape,
        scratch_shapes=scratch,
        compiler_params=_params("parallel", "parallel", "arbitrary"),
        name="proj_in",
    )(x, w)


def _out_mlp_body(o_ref, wo_ref, x_ref, g1_ref, b1_ref, w1_ref, w2_ref, g2_ref, b2_ref, y_ref,
                  x1_ref, xb_ref, acc_ref, *, nblk):
    k = pl.program_id(2)
    last = pl.num_programs(2) - 1
    tm = x1_ref.shape[0]
    ch = min(LN_ROWS, tm)
    chunks = [slice(c * ch, (c + 1) * ch) for c in range(tm // ch)]

    def hidden(xb):
        h = jnp.dot(xb, w1_ref[...], preferred_element_type=F32)
        h = jnp.square(jnp.maximum(h, 0.0)).astype(BF16)
        return jnp.dot(h, w2_ref[...], preferred_element_type=F32)

    @pl.when(k == 0)
    def _():
        for rows in chunks:
            o = jnp.concatenate([o_ref[0, n, rows, :] for n in range(nblk)], axis=-1)
            y = jnp.dot(o, wo_ref[...], preferred_element_type=F32)
            x1 = _layer_norm(ALPHA * x_ref[0, rows, :] + y, g1_ref[...], b1_ref[...])
            x1_ref[rows, :] = x1
            xb = x1.astype(BF16)
            xb_ref[rows, :] = xb
            acc_ref[rows, :] = hidden(xb)

    @pl.when((k > 0) & (k < last))
    def _():
        acc_ref[...] += hidden(xb_ref[...])

    @pl.when(k == last)
    def _():
        for rows in chunks:
            z = ALPHA * x1_ref[rows, :] + (acc_ref[rows, :] + hidden(xb_ref[rows, :]))
            y_ref[0, rows, :] = _layer_norm(z, g2_ref[...], b2_ref[...])


def _out_mlp(o, wo, x, g1, b1, w1, w2, g2, b2, *, tm=1024, tf=1024):
    B, nblk, S, wblk = o.shape
    D = x.shape[-1]
    FF = w1.shape[1]
    tm = min(tm, S)
    assert FF // tf >= 2
    row = lambda a: a.reshape(1, D)
    const = lambda shape: pl.BlockSpec(shape, lambda b, i, k: (0, 0))
    return pl.pallas_call(
        functools.partial(_out_mlp_body, nblk=nblk),
        grid=(B, S // tm, FF // tf),
        in_specs=[pl.BlockSpec((1, nblk, tm, wblk), lambda b, i, k: (b, 0, i, 0)),
                  const((nblk * wblk, D)),
                  pl.BlockSpec((1, tm, D), lambda b, i, k: (b, i, 0)),
                  const((1, D)), const((1, D)),
                  pl.BlockSpec((D, tf), lambda b, i, k: (0, k)),
                  pl.BlockSpec((tf, D), lambda b, i, k: (k, 0)),
                  const((1, D)), const((1, D))],
        out_specs=pl.BlockSpec((1, tm, D), lambda b, i, k: (b, i, 0)),
        out_shape=jax.ShapeDtypeStruct(x.shape, F32),
        scratch_shapes=[pltpu.VMEM((tm, D), F32), pltpu.VMEM((tm, D), BF16),
                        pltpu.VMEM((tm, D), F32)],
        compiler_params=_params("parallel", "parallel", "arbitrary"),
        name="out_mlp_ln",
    )(o, wo, x, row(g1), row(b1), w1, w2, row(g2), row(b2))


def _diff_body(slopes_ref, feat_ref, q_ref, k_ref, v_ref, lq1_ref, lk1_ref, lq2_ref, lk2_ref, g_ref,
               o_ref, vt_ref, kx_ref, mask_ref, qm_ref, sa_ref, sb_ref, pa_ref, pb_ref, acc_ref,
               *, tk, lambda_init):
    h = pl.program_id(1)
    d = DA_HEAD_DIM
    tq = 2 * tk
    S = k_ref.shape[3]
    lane = lax.broadcasted_iota(jnp.int32, (1, 2 * d), 1)
    digit = lane & (d - 1)
    full = slice(None)
    upper = slice(tk, tq)

    ch = min(512, S)
    for c in range(S // ch):
        sl = slice(c * ch, (c + 1) * ch)
        vt_ref[:2 * d, sl] = v_ref[0, 0, 0, sl, :].astype(F32).T.astype(BF16)
        feat = feat_ref[sl, :]
        k = k_ref[0, 0, 0, sl, :]
        kx_ref[0, sl, :] = jnp.where(lane < d, k, feat)
        kx_ref[1, sl, :] = jnp.where(lane >= d, k, feat)
    vt_ref[2 * d:, :] = jnp.ones((SUM_ROWS, S), BF16)
    rel = (lax.broadcasted_iota(jnp.int32, (tk, tq), 0)
           - lax.broadcasted_iota(jnp.int32, (tk, tq), 1))
    for c in range(2):
        mask_ref[c] = jnp.where(rel <= -c * tk, 0.0, NEG)

    sv = jnp.full((1, 2 * d), slopes_ref[h] * LOG2E, F32)
    s_a = sv.astype(BF16).astype(F32)
    s_b = (sv - s_a).astype(BF16).astype(F32)
    s_c = (sv - s_a - s_b).astype(BF16).astype(F32)
    piece = jnp.where((digit == 0) | (digit == 3), s_a,
                      jnp.where((digit == 1) | (digit == 4), s_b, s_c))
    qfeat = jnp.where(digit < 3, piece * (1 << POS_BITS), jnp.where(digit < 6, piece, 0.0))
    lam = (jnp.exp(jnp.sum(lq1_ref[...] * lk1_ref[...], axis=-1, keepdims=True))
           - jnp.exp(jnp.sum(lq2_ref[...] * lk2_ref[...], axis=-1, keepdims=True))
           + lambda_init)

    def scores(t, s_ref, qs=full):
        k0 = pl.multiple_of(t * tk, tk)
        cms = []
        for mp in range(2):
            s = _dot_nt(kx_ref[mp, pl.ds(k0, tk), :], qm_ref[mp, qs, :])
            s_ref[mp, :, qs] = s
            cms.append(jnp.max(s, axis=0, keepdims=True))
        return tuple(cms)

    def values(t, p_ref, a, qs=full):
        k0 = pl.multiple_of(t * tk, tk)
        vt = vt_ref[:, pl.ds(k0, tk)]
        for mp in range(2):
            acc_ref[mp, :, qs] = a[mp] * acc_ref[mp, :, qs] + jnp.dot(
                vt, p_ref[mp, :, qs], preferred_element_type=F32)

    def softmax(s_cur, p_cur, m, cm, mask=None, qs=full):
        m_new, a_new = [], []
        for mp in range(2):
            s = s_cur[mp, :, qs]
            if mask is None:
                c = cm[mp]
            else:
                s = s + mask[:, qs]
                c = jnp.max(s, axis=0, keepdims=True)
            mn = jnp.maximum(m[mp], c)
            a_new.append(jnp.exp2(m[mp] - mn))
            p_cur[mp, :, qs] = jnp.exp2(s - mn).astype(BF16)
            m_new.append(mn)
        return tuple(m_new), tuple(a_new)

    two = lambda x: (x, x)
    never = jnp.full((1, tk), NEG, F32)

    def query_tile(i, _):
        rows = pl.ds(pl.multiple_of(i * tq, tq), tq)
        last = 2 * i + 1
        q = q_ref[0, 0, 0, rows, :].astype(F32) * (d ** -0.5 * LOG2E)
        qm_ref[0] = jnp.where(lane < d, q, qfeat).astype(BF16)
        qm_ref[1] = jnp.where(lane >= d, q, qfeat).astype(BF16)
        acc_ref[...] = jnp.zeros_like(acc_ref)
        pb_ref[...] = jnp.zeros_like(pb_ref)

        def step(t, s_cur, s_nxt, p_cur, p_prev, carry):
            m, a, cm = carry
            m, a_new = softmax(s_cur, p_cur, m, cm)
            values(jnp.where(t == 0, last, t - 1), p_prev, a)
            return m, a_new, scores(t + 1, s_nxt)

        def pair(pp, carry):
            carry = step(2 * pp, sa_ref, sb_ref, pa_ref, pb_ref, carry)
            return step(2 * pp + 1, sb_ref, sa_ref, pb_ref, pa_ref, carry)

        cm = scores(0, sa_ref)
        scores(last, sb_ref, upper)
        m_up, _ = softmax(sb_ref, pb_ref, two(never), None, mask=mask_ref.at[1], qs=upper)
        m = tuple(jnp.concatenate([never, x], axis=1) for x in m_up)
        carry = (m, two(jnp.ones((1, tq), F32)), cm)
        m, a, cm = lax.fori_loop(0, i, pair, carry)
        m, a_d0 = softmax(sa_ref, pa_ref, m, cm, mask=mask_ref.at[0])
        values(jnp.where(i == 0, last, 2 * i - 1), pb_ref, a)
        values(2 * i, pa_ref, a_d0)

        o1 = acc_ref[0, :2 * d] * (1.0 / acc_ref[0, 2 * d:2 * d + 1])
        o2 = acc_ref[1, :2 * d] * (lam / acc_ref[1, 2 * d:2 * d + 1])
        o = o1 - o2
        o = o * lax.rsqrt(jnp.mean(o * o, axis=0, keepdims=True) + RMS_EPS)
        o_ref[0, 0, rows, :] = (o.T * g_ref[...] * (1.0 - lambda_init)).astype(o_ref.dtype)
        return 0

    lax.fori_loop(0, S // tq, query_tile, 0)


def _diff_attention(qkv, lq1, lk1, lq2, lk2, subln_g, lambda_init, *, tk=512):
    B, _, _, S, _ = qkv.shape
    H, d = DA_HEADS, DA_HEAD_DIM
    tk = min(tk, S // 2)
    tq = 2 * tk
    assert S % tq == 0 and S <= 1 << (2 * POS_BITS)
    slopes = 2.0 ** (-8.0 * jnp.arange(1, H + 1, dtype=F32) / H)
    pos = jnp.arange(S, dtype=jnp.int32)[:, None]
    slot = jnp.arange(2 * d, dtype=jnp.int32)[None, :] % d
    feat = jnp.where(slot < 3, pos >> POS_BITS,
                     jnp.where(slot < 6, pos & ((1 << POS_BITS) - 1), 0)).astype(BF16)
    vec = lambda a: a.reshape(1, -1).astype(F32)
    small = lambda n: pl.BlockSpec((1, n), lambda b, h: (0, 0))
    scores = pltpu.VMEM((2, tk, tq), F32)
    probs = pltpu.VMEM((2, tk, tq), BF16)
    return pl.pallas_call(
        functools.partial(_diff_body, tk=tk, lambda_init=lambda_init),
        grid=(B, H),
        in_specs=[pl.BlockSpec(memory_space=pltpu.SMEM),
                  pl.BlockSpec((S, 2 * d), lambda b, h: (0, 0)),
                  pl.BlockSpec((1, 1, 1, S, 2 * d), lambda b, h: (b, h, 0, 0, 0)),
                  pl.BlockSpec((1, 1, 1, S, 2 * d), lambda b, h: (b, H + h, 0, 0, 0)),
                  pl.BlockSpec((1, 1, 1, S, 2 * d), lambda b, h: (b, 2 * H + h, 0, 0, 0)),
                  small(d), small(d), small(d), small(d), small(2 * d)],
        out_specs=pl.BlockSpec((1, 1, S, 2 * d), lambda b, h: (b, h, 0, 0)),
        out_shape=jax.ShapeDtypeStruct((B, H, S, 2 * d), BF16),
        scratch_shapes=[pltpu.VMEM((2 * d + SUM_ROWS, S), BF16),
                        pltpu.VMEM((2, S, 2 * d), BF16),
                        pltpu.VMEM((2, tk, tq), F32),
                        pltpu.VMEM((2, tq, 2 * d), BF16),
                        scores, scores, probs, probs,
                        pltpu.VMEM((2, 2 * d + SUM_ROWS, tq), F32)],
        compiler_params=_params("parallel", "parallel"),
        name="diff_attn",
    )(slopes, feat, qkv, qkv, qkv, vec(lq1), vec(lk1), vec(lq2), vec(lk2), vec(subln_g))


def _dil_body(slopes_ref, *refs, tt, dils):
    G = len(dils)
    in_refs, o_ref = refs[:3 * G], refs[3 * G]
    og_refs = refs[3 * G + 1:3 * G + 1 + G]
    lse_refs = refs[3 * G + 1 + G:3 * G + 1 + 2 * G]
    bias_ref, tmp_ref = refs[3 * G + 1 + 2 * G:]
    h = pl.program_id(1)
    t = pl.program_id(2)
    slope = slopes_ref[h] * LOG2E
    blk = DIL_BLOCK
    scale = DIL_HEAD_DIM ** -0.5 * LOG2E
    base = (lax.broadcasted_iota(jnp.int32, (blk, 2 * blk), 0)
            - lax.broadcasted_iota(jnp.int32, (blk, 2 * blk), 1))

    for g in range(G):
        window = DIL_GROUPS[g][0] // dils[g]
        for first in range(2):
            dist = base + (0 if first else blk)
            bias_ref[2 * g + first] = jnp.where((dist >= 0) & (dist <= window),
                                                -(slope * dils[g]) * dist.astype(F32), NEG)

    for g in range(G):
        dil = dils[g]
        q_ref, k_ref, v_ref = in_refs[3 * g:3 * g + 3]
        rows = tt // dil
        nb = rows // blk

        def block(r, bi, dil=dil, q_ref=q_ref, k_ref=k_ref, v_ref=v_ref, rows=rows, g=g):
            lq = bi * blk
            l0 = t * rows + lq
            ks = pl.multiple_of(jnp.maximum(l0 - blk, 0), blk)
            qb = q_ref[0, 0, r, lq:lq + blk, :]
            kc = k_ref[0, 0, r, pl.ds(ks, 2 * blk), :]
            vc = v_ref[0, 0, r, pl.ds(ks, 2 * blk), :]
            s = _dot_nt(qb, kc) * scale + bias_ref[2 * g + (l0 == 0).astype(jnp.int32)]
            m = jnp.max(s, axis=-1, keepdims=True)
            p = jnp.exp2(s - m)
            l = jnp.sum(p, axis=-1, keepdims=True)
            acc = jnp.dot(p.astype(BF16), vc, preferred_element_type=F32)
            return acc * (1.0 / l), jnp.broadcast_to(m + jnp.log2(l), (blk, LANES))

        if dil <= SPLIT_STRIDE:
            for r in range(dil):
                for bi in range(nb):
                    tok = pl.ds(bi * blk * dil + r, blk, stride=dil)
                    og_refs[g][tok, :], lse_refs[g][tok, :] = block(r, bi)
        else:
            d2 = dil // SPLIT_STRIDE
            for r1 in range(SPLIT_STRIDE):
                for r2 in range(d2):
                    for bi in range(nb):
                        mid = pl.ds(bi * blk * d2 + r2, blk, stride=d2)
                        tmp_ref[0, r1, mid, :], tmp_ref[1, r1, mid, :] = block(
                            r2 * SPLIT_STRIDE + r1, bi)
                tok = pl.ds(r1, tt // SPLIT_STRIDE, stride=SPLIT_STRIDE)
                og_refs[g][tok, :] = tmp_ref[0, r1]
                lse_refs[g][tok, :] = tmp_ref[1, r1]

    ch = 256
    for c in range(tt // ch):
        sl = slice(c * ch, (c + 1) * ch)
        lses = [lse_refs[g][sl, :] for g in range(G)]
        top = functools.reduce(jnp.maximum, lses)
        ws = [jnp.exp2(lse - top) for lse in lses]
        num = sum(w * og_refs[g][sl, :] for g, w in enumerate(ws))
        o_ref[0, 0, sl, :] = (num / sum(ws)).astype(o_ref.dtype)


def _dilated_attention(qkvs, *, tt=2048):
    dils = tuple(d for _, d in DIL_GROUPS)
    B = qkvs[0].shape[0]
    S = qkvs[0].shape[2] * qkvs[0].shape[3]
    H, dh = DIL_HEADS, DIL_HEAD_DIM
    tt = min(tt, S)
    assert all(tt % (d * DIL_BLOCK) == 0 and S // d >= 2 * DIL_BLOCK for d in dils)
    slopes = 2.0 ** (-8.0 * jnp.arange(1, H + 1, dtype=F32) / H)
    in_specs = [pl.BlockSpec(memory_space=pltpu.SMEM)]
    args = [slopes]
    for a, dil in zip(qkvs, dils):
        L = S // dil
        in_specs += [
            pl.BlockSpec((1, 1, dil, tt // dil, dh), lambda b, h, t: (b, h, 0, t, 0)),
            pl.BlockSpec((1, 1, dil, L, dh), lambda b, h, t: (b, H + h, 0, 0, 0)),
            pl.BlockSpec((1, 1, dil, L, dh), lambda b, h, t: (b, 2 * H + h, 0, 0, 0))]
        args += [a, a, a]
    return pl.pallas_call(
        functools.partial(_dil_body, tt=tt, dils=dils),
        grid=(B, H, S // tt),
        in_specs=in_specs,
        out_specs=pl.BlockSpec((1, 1, tt, dh), lambda b, h, t: (b, h, t, 0)),
        out_shape=jax.ShapeDtypeStruct((B, H, S, dh), BF16),
        scratch_shapes=([pltpu.VMEM((tt, LANES), F32)] * (2 * len(dils))
                        + [pltpu.VMEM((2 * len(dils), DIL_BLOCK, 2 * DIL_BLOCK), F32),
                           pltpu.VMEM((2, SPLIT_STRIDE, tt // SPLIT_STRIDE, LANES), F32)]),
        compiler_params=_params("parallel", "parallel", "arbitrary"),
        name="dilated_attn",
    )(*args)


def _gate_body(x_ref, wg_ref, w2_ref, b_ref, o_ref):
    C = GLA_CHUNK
    g_low = jnp.dot(x_ref[0].astype(BF16), wg_ref[...], preferred_element_type=F32)
    z = jnp.dot(g_low.astype(BF16), w2_ref[...], preferred_element_type=F32) + b_ref[...]
    log_a = (jnp.minimum(z, 0.0) - jnp.log1p(jnp.exp(-jnp.abs(z)))) / GLA_TAU
    tri = (lax.broadcasted_iota(jnp.int32, (C, C), 0)
           >= lax.broadcasted_iota(jnp.int32, (C, C), 1)).astype(BF16)
    for c in range(log_a.shape[0] // C):
        a = log_a[c * C:(c + 1) * C]
        hi = a.astype(BF16)
        rem = a - hi.astype(F32)
        mid = rem.astype(BF16)
        lo = (rem - mid.astype(F32)).astype(BF16)
        cum = (jnp.dot(tri, hi, preferred_element_type=F32)
               + jnp.dot(tri, mid, preferred_element_type=F32)
               + jnp.dot(tri, lo, preferred_element_type=F32))
        o_ref[0, c * C:(c + 1) * C, :] = cum


def _gla_gate(x, wg, w2, b, *, tm=512):
    B, S, D = x.shape
    tm = min(tm, S)
    return pl.pallas_call(
        _gate_body,
        grid=(B, S // tm),
        in_specs=[pl.BlockSpec((1, tm, D), lambda b_, i: (b_, i, 0)),
                  pl.BlockSpec((D, LANES), lambda b_, i: (0, 0)),
                  pl.BlockSpec((LANES, GLA_DK), lambda b_, i: (0, 0)),
                  pl.BlockSpec((1, GLA_DK), lambda b_, i: (0, 0))],
        out_specs=pl.BlockSpec((1, tm, GLA_DK), lambda b_, i: (b_, i, 0)),
        out_shape=jax.ShapeDtypeStruct((B, S, GLA_DK), F32),
        compiler_params=_params("parallel", "parallel"),
        name="gla_gate",
    )(x, wg, w2, b.reshape(1, GLA_DK))


def _gla_body(q_ref, k_ref, v_ref, r_ref, b_ref, g_ref, o_ref, st_ref, *, hp):
    C = GLA_CHUNK
    dk, dv = GLA_DK // GLA_HEADS, GLA_DV // GLA_HEADS

    @pl.when(pl.program_id(2) == 0)
    def _():
        st_ref[...] = jnp.zeros_like(st_ref)

    causal = (lax.broadcasted_iota(jnp.int32, (C, C), 0)
              >= lax.broadcasted_iota(jnp.int32, (C, C), 1))
    st = [st_ref[hh] for hh in range(hp)]
    for c in range(q_ref.shape[1] // C):
        sl = slice(c * C, (c + 1) * C)
        for hh in range(hp):
            ck = slice(hh * dk, (hh + 1) * dk)
            cv = slice(hh * dv, (hh + 1) * dv)
            bc = b_ref[0, sl, ck]
            bl = bc[C - 1:C, :]
            kk = k_ref[0, sl, ck]
            q_dec = (q_ref[0, sl, ck] * (dk ** -0.5) * jnp.exp(bc)).astype(BF16)
            k_intra = (kk * jnp.exp(-bc)).astype(BF16)
            k_state = (kk * jnp.exp(bl - bc)).astype(BF16)
            vb = v_ref[0, sl, cv].astype(BF16)
            s = jnp.where(causal, _dot_nt(q_dec, k_intra), 0.0)
            o = (jnp.dot(s.astype(BF16), vb, preferred_element_type=F32)
                 + _dot_nt(q_dec, st[hh].astype(BF16)))
            st[hh] = st[hh] * jnp.exp(bl) + lax.dot_general(
                vb, k_state, (((0,), (0,)), ((), ())), preferred_element_type=F32)
            o = o * lax.rsqrt(jnp.mean(o * o, axis=-1, keepdims=True) + RMS_EPS) * g_ref[...]
            r = r_ref[0, sl, cv]
            o_ref[0, hh, sl, :] = (o * (r * jax.nn.sigmoid(r))).astype(o_ref.dtype)
    for hh in range(hp):
        st_ref[hh] = st[hh]


def _gla(proj, bcum, gnorm_g, *, ts=512, hp=4):
    B, S, _ = proj.shape
    H = GLA_HEADS
    dk, dv = GLA_DK // H, GLA_DV // H
    ts = min(ts, S)
    wk, wv = hp * dk, hp * dv
    nk = GLA_DK // wk
    nv = 2 * GLA_DK // wv
    return pl.pallas_call(
        functools.partial(_gla_body, hp=hp),
        grid=(B, H // hp, S // ts),
        in_specs=[pl.BlockSpec((1, ts, wk), lambda b, h, t: (b, t, h)),
                  pl.BlockSpec((1, ts, wk), lambda b, h, t: (b, t, nk + h)),
                  pl.BlockSpec((1, ts, wv), lambda b, h, t: (b, t, nv + h)),
                  pl.BlockSpec((1, ts, wv), lambda b, h, t: (b, t, nv + H // hp + h)),
                  pl.BlockSpec((1, ts, wk), lambda b, h, t: (b, t, h)),
                  pl.BlockSpec((1, dv), lambda b, h, t: (0, 0))],
        out_specs=pl.BlockSpec((1, hp, ts, dv), lambda b, h, t: (b, h, t, 0)),
        out_shape=jax.ShapeDtypeStruct((B, H, S, dv), BF16),
        scratch_shapes=[pltpu.VMEM((hp, dv, dk), F32)],
        compiler_params=_params("parallel", "parallel", "arbitrary"),
        name="gla",
    )(proj, proj, proj, proj, bcum, gnorm_g.reshape(1, dv))


def _diff_mixer(x, w_in, lq1, lk1, lq2, lk2, subln_g, layer_idx):
    lambda_init = 0.8 - 0.6 * math.exp(-0.3 * layer_idx)
    qkv = _proj(x, w_in.astype(BF16), tn=3072)
    return _diff_attention(qkv, lq1, lk1, lq2, lk2, subln_g, lambda_init)


def _dil_mixer(x, w_in):
    wb = w_in.astype(BF16)
    per_group = 3 * DIL_HEADS * DIL_HEAD_DIM
    qkvs = [_proj(x, wb[:, g * per_group:(g + 1) * per_group], dil=dil,
                  tn=per_group if dil == 1 else per_group // 2)
            for g, (_, dil) in enumerate(DIL_GROUPS)]
    return _dilated_attention(qkvs)


def _gla_mixer(x, w_in, w_gate2, b_gate, gnorm_g):
    n_main = 2 * GLA_DK + 2 * GLA_DV
    wb = w_in.astype(BF16)
    proj = _proj(x, wb[:, :n_main], head_major=False, out_dtype=F32, tn=1536)
    wg = jnp.pad(wb[:, n_main:], ((0, 0), (0, LANES - GLA_GATE_RANK)))
    w2 = jnp.pad(w_gate2.astype(BF16), ((0, LANES - GLA_GATE_RANK), (0, 0)))
    bcum = _gla_gate(x, wg, w2, b_gate)
    return _gla(proj, bcum, gnorm_g)


def _finish_layer(o, x, w_out, g1, b1, w1, w2, g2, b2):
    return _out_mlp(o, w_out.astype(BF16), x, g1, b1, w1.astype(BF16), w2.astype(BF16), g2, b2)


def kernel(x, l0_w_in, l0_lam_q1, l0_lam_k1, l0_lam_q2, l0_lam_k2, l0_subln_g, l0_w_out, l0_ln1_g, l0_ln1_b, l0_w_ff1, l0_w_ff2, l0_ln2_g, l0_ln2_b, l1_w_in, l1_w_out, l1_ln1_g, l1_ln1_b, l1_w_ff1, l1_w_ff2, l1_ln2_g, l1_ln2_b, l2_w_in, l2_w_gate2, l2_b_gate, l2_gnorm_g, l2_w_out, l2_ln1_g, l2_ln1_b, l2_w_ff1, l2_w_ff2, l2_ln2_g, l2_ln2_b, l3_w_in, l3_lam_q1, l3_lam_k1, l3_lam_q2, l3_lam_k2, l3_subln_g, l3_w_out, l3_ln1_g, l3_ln1_b, l3_w_ff1, l3_w_ff2, l3_ln2_g, l3_ln2_b):
    o = _diff_mixer(x, l0_w_in, l0_lam_q1, l0_lam_k1, l0_lam_q2, l0_lam_k2, l0_subln_g, 0)
    x = _finish_layer(o, x, l0_w_out, l0_ln1_g, l0_ln1_b, l0_w_ff1, l0_w_ff2, l0_ln2_g, l0_ln2_b)
    o = _dil_mixer(x, l1_w_in)
    x = _finish_layer(o, x, l1_w_out, l1_ln1_g, l1_ln1_b, l1_w_ff1, l1_w_ff2, l1_ln2_g, l1_ln2_b)
    o = _gla_mixer(x, l2_w_in, l2_w_gate2, l2_b_gate, l2_gnorm_g)
    x = _finish_layer(o, x, l2_w_out, l2_ln1_g, l2_ln1_b, l2_w_ff1, l2_w_ff2, l2_ln2_g, l2_ln2_b)
    o = _diff_mixer(x, l3_w_in, l3_lam_q1, l3_lam_k1, l3_lam_q2, l3_lam_k2, l3_subln_g, 3)
    x = _finish_layer(o, x, l3_w_out, l3_ln1_g, l3_ln1_b, l3_w_ff1, l3_w_ff2, l3_ln2_g, l3_ln2_b)
    return x
```

```python
import functools
import math

import jax
import jax.numpy as jnp
from jax import lax
from jax.experimental import pallas as pl
from jax.experimental.pallas import tpu as pltpu

F32 = jnp.float32
BF16 = jnp.bfloat16

D_MODEL = 1024
DEPTH = 4
N_MIXERS = 3
DA_HEADS = 8
DA_HEAD_DIM = 64
DIL_GROUPS = ((128, 1), (512, 4), (2048, 16))
DIL_HEADS = 8
DIL_HEAD_DIM = 128
DIL_BLOCK = 128
GLA_HEADS = 4
GLA_DK = D_MODEL // 2
GLA_DV = D_MODEL
GLA_GATE_RANK = 16
GLA_TAU = 16.0
GLA_CHUNK = 64
D_FF = 4 * D_MODEL
ALPHA = (2 * DEPTH) ** 0.25
LN_EPS = 1e-5
RMS_EPS = 1e-6

LANES = 128
VMEM_LIMIT = 56 * 1024 * 1024
NEG = -1e30
LOG2E = math.log2(math.e)
SUM_ROWS = 16
SPLIT_STRIDE = 4
LN_ROWS = 512
POS_BITS = 6


def _params(*sem):
    return pltpu.CompilerParams(dimension_semantics=sem, vmem_limit_bytes=VMEM_LIMIT)


def _layer_norm(z, g, b):
    mu = jnp.mean(z, axis=-1, keepdims=True)
    zc = z - mu
    var = jnp.mean(zc * zc, axis=-1, keepdims=True)
    return zc * lax.rsqrt(var + LN_EPS) * g + b


def _dot_nt(a, b):
    return lax.dot_general(a, b, (((1,), (1,)), ((), ())), preferred_element_type=F32)


def _proj_body(x_ref, w_ref, o_ref, xb_ref, *acc_scratch, dil, head_major):
    @pl.when(pl.program_id(2) == 0)
    def _():
        xb_ref[...] = x_ref[0].astype(BF16)

    acc = jnp.dot(xb_ref[...], w_ref[...], preferred_element_type=F32)
    tm, tn = acc.shape
    if not head_major:
        o_ref[0] = acc.astype(o_ref.dtype)
    elif dil == 1:
        for c in range(tn // LANES):
            o_ref[0, c, 0] = acc[:, c * LANES:(c + 1) * LANES].astype(o_ref.dtype)
    else:
        acc_ref = acc_scratch[0]
        rows = tm // dil
        for c in range(tn // LANES):
            acc_ref[c] = acc[:, c * LANES:(c + 1) * LANES]
            if dil <= SPLIT_STRIDE:
                for r in range(dil):
                    o_ref[0, c, r] = acc_ref[c, pl.ds(r, rows, stride=dil), :].astype(o_ref.dtype)
            else:
                tmp_ref = acc_scratch[1]
                d2 = dil // SPLIT_STRIDE
                for r1 in range(SPLIT_STRIDE):
                    tmp_ref[r1] = acc_ref[c, pl.ds(r1, tm // SPLIT_STRIDE, stride=SPLIT_STRIDE), :]
                    for r2 in range(d2):
                        o_ref[0, c, r2 * SPLIT_STRIDE + r1] = tmp_ref[
                            r1, pl.ds(r2, rows, stride=d2), :].astype(o_ref.dtype)


def _proj(x, w, *, dil=1, head_major=True, out_dtype=BF16, tm=1024, tn=1024):
    B, S, D = x.shape
    N = w.shape[1]
    tm = min(tm, S)
    tn = min(tn, N)
    assert S % tm == 0 and N % tn == 0 and tm % dil == 0 and tn % LANES == 0
    if head_major:
        out_shape = jax.ShapeDtypeStruct((B, N // LANES, dil, S // dil, LANES), out_dtype)
        out_spec = pl.BlockSpec((1, tn // LANES, dil, tm // dil, LANES),
                                lambda b, i, j: (b, j, 0, i, 0))
    else:
        out_shape = jax.ShapeDtypeStruct((B, S, N), out_dtype)
        out_spec = pl.BlockSpec((1, tm, tn), lambda b, i, j: (b, i, j))
    scratch = [pltpu.VMEM((tm, D), BF16)]
    if head_major and dil > 1:
        scratch.append(pltpu.VMEM((tn // LANES, tm, LANES), F32))
        if dil > SPLIT_STRIDE:
            scratch.append(pltpu.VMEM((SPLIT_STRIDE, tm // SPLIT_STRIDE, LANES), F32))
    return pl.pallas_call(
        functools.partial(_proj_body, dil=dil, head_major=head_major),
        grid=(B, S // tm, N // tn),
        in_specs=[pl.BlockSpec((1, tm, D), lambda b, i, j: (b, i, 0)),
                  pl.BlockSpec((D, tn), lambda b, i, j: (0, j))],
        out_specs=out_spec,
        out_shape=out_shape,
        scratch_shapes=scratch,
        compiler_params=_params("parallel", "parallel", "arbitrary"),
        name="proj_in",
    )(x, w)


def _out_mlp_body(o_ref, wo_ref, x_ref, g1_ref, b1_ref, w1_ref, w2_ref, g2_ref, b2_ref, y_ref,
                  x1_ref, xb_ref, acc_ref, *, nblk):
    k = pl.program_id(2)
    last = pl.num_programs(2) - 1
    tm = x1_ref.shape[0]
    ch = min(LN_ROWS, tm)
    chunks = [slice(c * ch, (c + 1) * ch) for c in range(tm // ch)]

    def hidden(xb):
        h = jnp.dot(xb, w1_ref[...], preferred_element_type=F32)
        h = jnp.square(jnp.maximum(h, 0.0)).astype(BF16)
        return jnp.dot(h, w2_ref[...], preferred_element_type=F32)

    @pl.when(k == 0)
    def _():
        for rows in chunks:
            o = jnp.concatenate([o_ref[0, n, rows, :] for n in range(nblk)], axis=-1)
            y = jnp.dot(o, wo_ref[...], preferred_element_type=F32)
            x1 = _layer_norm(ALPHA * x_ref[0, rows, :] + y, g1_ref[...], b1_ref[...])
            x1_ref[rows, :] = x1
            xb = x1.astype(BF16)
            xb_ref[rows, :] = xb
            acc_ref[rows, :] = hidden(xb)

    @pl.when((k > 0) & (k < last))
    def _():
        acc_ref[...] += hidden(xb_ref[...])

    @pl.when(k == last)
    def _():
        for rows in chunks:
            z = ALPHA * x1_ref[rows, :] + (acc_ref[rows, :] + hidden(xb_ref[rows, :]))
            y_ref[0, rows, :] = _layer_norm(z, g2_ref[...], b2_ref[...])


def _out_mlp(o, wo, x, g1, b1, w1, w2, g2, b2, *, tm=1024, tf=1024):
    B, nblk, S, wblk = o.shape
    D = x.shape[-1]
    FF = w1.shape[1]
    tm = min(tm, S)
    assert FF // tf >= 2
    row = lambda a: a.reshape(1, D)
    const = lambda shape: pl.BlockSpec(shape, lambda b, i, k: (0, 0))
    return pl.pallas_call(
        functools.partial(_out_mlp_body, nblk=nblk),
        grid=(B, S // tm, FF // tf),
        in_specs=[pl.BlockSpec((1, nblk, tm, wblk), lambda b, i, k: (b, 0, i, 0)),
                  const((nblk * wblk, D)),
                  pl.BlockSpec((1, tm, D), lambda b, i, k: (b, i, 0)),
                  const((1, D)), const((1, D)),
                  pl.BlockSpec((D, tf), lambda b, i, k: (0, k)),
                  pl.BlockSpec((tf, D), lambda b, i, k: (k, 0)),
                  const((1, D)), const((1, D))],
        out_specs=pl.BlockSpec((1, tm, D), lambda b, i, k: (b, i, 0)),
        out_shape=jax.ShapeDtypeStruct(x.shape, F32),
        scratch_shapes=[pltpu.VMEM((tm, D), F32), pltpu.VMEM((tm, D), BF16),
                        pltpu.VMEM((tm, D), F32)],
        compiler_params=_params("parallel", "parallel", "arbitrary"),
        name="out_mlp_ln",
    )(o, wo, x, row(g1), row(b1), w1, w2, row(g2), row(b2))


def _diff_body(slopes_ref, feat_ref, q_ref, k_ref, v_ref, lq1_ref, lk1_ref, lq2_ref, lk2_ref, g_ref,
               o_ref, vt_ref, kx_ref, mask_ref, qm_ref, sa_ref, sb_ref, pa_ref, pb_ref, acc_ref,
               *, tk, lambda_init):
    h = pl.program_id(1)
    d = DA_HEAD_DIM
    tq = 2 * tk
    S = k_ref.shape[3]
    lane = lax.broadcasted_iota(jnp.int32, (1, 2 * d), 1)
    digit = lane & (d - 1)
    full = slice(None)
    upper = slice(tk, tq)

    ch = min(512, S)
    for c in range(S // ch):
        sl = slice(c * ch, (c + 1) * ch)
        vt_ref[:2 * d, sl] = v_ref[0, 0, 0, sl, :].astype(F32).T.astype(BF16)
        feat = feat_ref[sl, :]
        k = k_ref[0, 0, 0, sl, :]
        kx_ref[0, sl, :] = jnp.where(lane < d, k, feat)
        kx_ref[1, sl, :] = jnp.where(lane >= d, k, feat)
    vt_ref[2 * d:, :] = jnp.ones((SUM_ROWS, S), BF16)
    rel = (lax.broadcasted_iota(jnp.int32, (tk, tq), 0)
           - lax.broadcasted_iota(jnp.int32, (tk, tq), 1))
    for c in range(2):
        mask_ref[c] = jnp.where(rel <= -c * tk, 0.0, NEG)

    sv = jnp.full((1, 2 * d), slopes_ref[h] * LOG2E, F32)
    s_a = sv.astype(BF16).astype(F32)
    s_b = (sv - s_a).astype(BF16).astype(F32)
    s_c = (sv - s_a - s_b).astype(BF16).astype(F32)
    piece = jnp.where((digit == 0) | (digit == 3), s_a,
                      jnp.where((digit == 1) | (digit == 4), s_b, s_c))
    qfeat = jnp.where(digit < 3, piece * (1 << POS_BITS), jnp.where(digit < 6, piece, 0.0))
    lam = (jnp.exp(jnp.sum(lq1_ref[...] * lk1_ref[...], axis=-1, keepdims=True))
           - jnp.exp(jnp.sum(lq2_ref[...] * lk2_ref[...], axis=-1, keepdims=True))
           + lambda_init)

    def scores(t, s_ref, qs=full):
        k0 = pl.multiple_of(t * tk, tk)
        cms = []
        for mp in range(2):
            s = _dot_nt(kx_ref[mp, pl.ds(k0, tk), :], qm_ref[mp, qs, :])
            s_ref[mp, :, qs] = s
            cms.append(jnp.max(s, axis=0, keepdims=True))
        return tuple(cms)

    def values(t, p_ref, a, qs=full):
        k0 = pl.multiple_of(t * tk, tk)
        vt = vt_ref[:, pl.ds(k0, tk)]
        for mp in range(2):
            acc_ref[mp, :, qs] = a[mp] * acc_ref[mp, :, qs] + jnp.dot(
                vt, p_ref[mp, :, qs], preferred_element_type=F32)

    def softmax(s_cur, p_cur, m, cm, mask=None, qs=full):
        m_new, a_new = [], []
        for mp in range(2):
            s = s_cur[mp, :, qs]
            if mask is None:
                c = cm[mp]
            else:
                s = s + mask[:, qs]
                c = jnp.max(s, axis=0, keepdims=True)
            mn = jnp.maximum(m[mp], c)
            a_new.append(jnp.exp2(m[mp] - mn))
            p_cur[mp, :, qs] = jnp.exp2(s - mn).astype(BF16)
            m_new.append(mn)
        return tuple(m_new), tuple(a_new)

    two = lambda x: (x, x)
    never = jnp.full((1, tk), NEG, F32)

    def query_tile(i, _):
        rows = pl.ds(pl.multiple_of(i * tq, tq), tq)
        last = 2 * i + 1
        q = q_ref[0, 0, 0, rows, :].astype(F32) * (d ** -0.5 * LOG2E)
        qm_ref[0] = jnp.where(lane < d, q, qfeat).astype(BF16)
        qm_ref[1] = jnp.where(lane >= d, q, qfeat).astype(BF16)
        acc_ref[...] = jnp.zeros_like(acc_ref)
        pb_ref[...] = jnp.zeros_like(pb_ref)

        def step(t, s_cur, s_nxt, p_cur, p_prev, carry):
            m, a, cm = carry
            m, a_new = softmax(s_cur, p_cur, m, cm)
            values(t, p_cur, a_new)
            return m, a_new, scores(t + 1, s_nxt)

        def pair(pp, carry):
            carry = step(2 * pp, sa_ref, sb_ref, pa_ref, pb_ref, carry)
            return step(2 * pp + 1, sb_ref, sa_ref, pb_ref, pa_ref, carry)

        cm = scores(0, sa_ref)
        scores(last, sb_ref, upper)
        m_up, a_up = softmax(sb_ref, pb_ref, two(never), None, mask=mask_ref.at[1], qs=upper)
        values(last, pb_ref, a_up, upper)
        m = tuple(jnp.concatenate([never, x], axis=1) for x in m_up)
        carry = (m, two(jnp.ones((1, tq), F32)), cm)
        m, a, cm = lax.fori_loop(0, i, pair, carry)
        m, a_d0 = softmax(sa_ref, pa_ref, m, cm, mask=mask_ref.at[0])
        values(2 * i, pa_ref, a_d0)

        o1 = acc_ref[0, :2 * d] * (1.0 / acc_ref[0, 2 * d:2 * d + 1])
        o2 = acc_ref[1, :2 * d] * (lam / acc_ref[1, 2 * d:2 * d + 1])
        o = o1 - o2
        o = o * lax.rsqrt(jnp.mean(o * o, axis=0, keepdims=True) + RMS_EPS)
        o_ref[0, 0, rows, :] = (o.T * g_ref[...] * (1.0 - lambda_init)).astype(o_ref.dtype)
        return 0

    lax.fori_loop(0, S // tq, query_tile, 0)


def _diff_attention(qkv, lq1, lk1, lq2, lk2, subln_g, lambda_init, *, tk=512):
    B, _, _, S, _ = qkv.shape
    H, d = DA_HEADS, DA_HEAD_DIM
    tk = min(tk, S // 2)
    tq = 2 * tk
    assert S % tq == 0 and S <= 1 << (2 * POS_BITS)
    slopes = 2.0 ** (-8.0 * jnp.arange(1, H + 1, dtype=F32) / H)
    pos = jnp.arange(S, dtype=jnp.int32)[:, None]
    slot = jnp.arange(2 * d, dtype=jnp.int32)[None, :] % d
    feat = jnp.where(slot < 3, pos >> POS_BITS,
                     jnp.where(slot < 6, pos & ((1 << POS_BITS) - 1), 0)).astype(BF16)
    vec = lambda a: a.reshape(1, -1).astype(F32)
    small = lambda n: pl.BlockSpec((1, n), lambda b, h: (0, 0))
    scores = pltpu.VMEM((2, tk, tq), F32)
    probs = pltpu.VMEM((2, tk, tq), BF16)
    return pl.pallas_call(
        functools.partial(_diff_body, tk=tk, lambda_init=lambda_init),
        grid=(B, H),
        in_specs=[pl.BlockSpec(memory_space=pltpu.SMEM),
                  pl.BlockSpec((S, 2 * d), lambda b, h: (0, 0)),
                  pl.BlockSpec((1, 1, 1, S, 2 * d), lambda b, h: (b, h, 0, 0, 0)),
                  pl.BlockSpec((1, 1, 1, S, 2 * d), lambda b, h: (b, H + h, 0, 0, 0)),
                  pl.BlockSpec((1, 1, 1, S, 2 * d), lambda b, h: (b, 2 * H + h, 0, 0, 0)),
                  small(d), small(d), small(d), small(d), small(2 * d)],
        out_specs=pl.BlockSpec((1, 1, S, 2 * d), lambda b, h: (b, h, 0, 0)),
        out_shape=jax.ShapeDtypeStruct((B, H, S, 2 * d), BF16),
        scratch_shapes=[pltpu.VMEM((2 * d + SUM_ROWS, S), BF16),
                        pltpu.VMEM((2, S, 2 * d), BF16),
                        pltpu.VMEM((2, tk, tq), F32),
                        pltpu.VMEM((2, tq, 2 * d), BF16),
                        scores, scores, probs, probs,
                        pltpu.VMEM((2, 2 * d + SUM_ROWS, tq), F32)],
        compiler_params=_params("parallel", "parallel"),
        name="diff_attn",
    )(slopes, feat, qkv, qkv, qkv, vec(lq1), vec(lk1), vec(lq2), vec(lk2), vec(subln_g))


def _dil_body(slopes_ref, *refs, tt, dils):
    G = len(dils)
    in_refs, o_ref = refs[:3 * G], refs[3 * G]
    og_refs = refs[3 * G + 1:3 * G + 1 + G]
    lse_refs = refs[3 * G + 1 + G:3 * G + 1 + 2 * G]
    bias_ref, tmp_ref = refs[3 * G + 1 + 2 * G:]
    h = pl.program_id(1)
    t = pl.program_id(2)
    slope = slopes_ref[h] * LOG2E
    blk = DIL_BLOCK
    scale = DIL_HEAD_DIM ** -0.5 * LOG2E
    base = (lax.broadcasted_iota(jnp.int32, (blk, 2 * blk), 0)
            - lax.broadcasted_iota(jnp.int32, (blk, 2 * blk), 1))

    for g in range(G):
        window = DIL_GROUPS[g][0] // dils[g]
        for first in range(2):
            dist = base + (0 if first else blk)
            bias_ref[2 * g + first] = jnp.where((dist >= 0) & (dist <= window),
                                                -(slope * dils[g]) * dist.astype(F32), NEG)

    for g in range(G):
        dil = dils[g]
        q_ref, k_ref, v_ref = in_refs[3 * g:3 * g + 3]
        rows = tt // dil
        nb = rows // blk

        def block(r, bi, dil=dil, q_ref=q_ref, k_ref=k_ref, v_ref=v_ref, rows=rows, g=g):
            lq = bi * blk
            l0 = t * rows + lq
            ks = pl.multiple_of(jnp.maximum(l0 - blk, 0), blk)
            qb = q_ref[0, 0, r, lq:lq + blk, :]
            kc = k_ref[0, 0, r, pl.ds(ks, 2 * blk), :]
            vc = v_ref[0, 0, r, pl.ds(ks, 2 * blk), :]
            s = _dot_nt(qb, kc) * scale + bias_ref[2 * g + (l0 == 0).astype(jnp.int32)]
            m = jnp.max(s, axis=-1, keepdims=True)
            p = jnp.exp2(s - m)
            l = jnp.sum(p, axis=-1, keepdims=True)
            acc = jnp.dot(p.astype(BF16), vc, preferred_element_type=F32)
            return acc * (1.0 / l), jnp.broadcast_to(m + jnp.log2(l), (blk, LANES))

        if dil <= SPLIT_STRIDE:
            for r in range(dil):
                for bi in range(nb):
                    tok = pl.ds(bi * blk * dil + r, blk, stride=dil)
                    og_refs[g][tok, :], lse_refs[g][tok, :] = block(r, bi)
        else:
            d2 = dil // SPLIT_STRIDE
            for r1 in range(SPLIT_STRIDE):
                for r2 in range(d2):
                    for bi in range(nb):
                        mid = pl.ds(bi * blk * d2 + r2, blk, stride=d2)
                        tmp_ref[0, r1, mid, :], tmp_ref[1, r1, mid, :] = block(
                            r2 * SPLIT_STRIDE + r1, bi)
                tok = pl.ds(r1, tt // SPLIT_STRIDE, stride=SPLIT_STRIDE)
                og_refs[g][tok, :] = tmp_ref[0, r1]
                lse_refs[g][tok, :] = tmp_ref[1, r1]

    ch = 256
    for c in range(tt // ch):
        sl = slice(c * ch, (c + 1) * ch)
        lses = [lse_refs[g][sl, :] for g in range(G)]
        top = functools.reduce(jnp.maximum, lses)
        ws = [jnp.exp2(lse - top) for lse in lses]
        num = sum(w * og_refs[g][sl, :] for g, w in enumerate(ws))
        o_ref[0, 0, sl, :] = (num / sum(ws)).astype(o_ref.dtype)


def _dilated_attention(qkvs, *, tt=2048):
    dils = tuple(d for _, d in DIL_GROUPS)
    B = qkvs[0].shape[0]
    S = qkvs[0].shape[2] * qkvs[0].shape[3]
    H, dh = DIL_HEADS, DIL_HEAD_DIM
    tt = min(tt, S)
    assert all(tt % (d * DIL_BLOCK) == 0 and S // d >= 2 * DIL_BLOCK for d in dils)
    slopes = 2.0 ** (-8.0 * jnp.arange(1, H + 1, dtype=F32) / H)
    in_specs = [pl.BlockSpec(memory_space=pltpu.SMEM)]
    args = [slopes]
    for a, dil in zip(qkvs, dils):
        L = S // dil
        in_specs += [
            pl.BlockSpec((1, 1, dil, tt // dil, dh), lambda b, h, t: (b, h, 0, t, 0)),
            pl.BlockSpec((1, 1, dil, L, dh), lambda b, h, t: (b, H + h, 0, 0, 0)),
            pl.BlockSpec((1, 1, dil, L, dh), lambda b, h, t: (b, 2 * H + h, 0, 0, 0))]
        args += [a, a, a]
    return pl.pallas_call(
        functools.partial(_dil_body, tt=tt, dils=dils),
        grid=(B, H, S // tt),
        in_specs=in_specs,
        out_specs=pl.BlockSpec((1, 1, tt, dh), lambda b, h, t: (b, h, t, 0)),
        out_shape=jax.ShapeDtypeStruct((B, H, S, dh), BF16),
        scratch_shapes=([pltpu.VMEM((tt, LANES), F32)] * (2 * len(dils))
                        + [pltpu.VMEM((2 * len(dils), DIL_BLOCK, 2 * DIL_BLOCK), F32),
                           pltpu.VMEM((2, SPLIT_STRIDE, tt // SPLIT_STRIDE, LANES), F32)]),
        compiler_params=_params("parallel", "parallel", "arbitrary"),
        name="dilated_attn",
    )(*args)


def _gate_body(x_ref, wg_ref, w2_ref, b_ref, o_ref):
    C = GLA_CHUNK
    g_low = jnp.dot(x_ref[0].astype(BF16), wg_ref[...], preferred_element_type=F32)
    z = jnp.dot(g_low.astype(BF16), w2_ref[...], preferred_element_type=F32) + b_ref[...]
    log_a = (jnp.minimum(z, 0.0) - jnp.log1p(jnp.exp(-jnp.abs(z)))) / GLA_TAU
    tri = (lax.broadcasted_iota(jnp.int32, (C, C), 0)
           >= lax.broadcasted_iota(jnp.int32, (C, C), 1)).astype(BF16)
    for c in range(log_a.shape[0] // C):
        a = log_a[c * C:(c + 1) * C]
        hi = a.astype(BF16)
        rem = a - hi.astype(F32)
        mid = rem.astype(BF16)
        lo = (rem - mid.astype(F32)).astype(BF16)
        cum = (jnp.dot(tri, hi, preferred_element_type=F32)
               + jnp.dot(tri, mid, preferred_element_type=F32)
               + jnp.dot(tri, lo, preferred_element_type=F32))
        o_ref[0, c * C:(c + 1) * C, :] = cum


def _gla_gate(x, wg, w2, b, *, tm=512):
    B, S, D = x.shape
    tm = min(tm, S)
    return pl.pallas_call(
        _gate_body,
        grid=(B, S // tm),
        in_specs=[pl.BlockSpec((1, tm, D), lambda b_, i: (b_, i, 0)),
                  pl.BlockSpec((D, LANES), lambda b_, i: (0, 0)),
                  pl.BlockSpec((LANES, GLA_DK), lambda b_, i: (0, 0)),
                  pl.BlockSpec((1, GLA_DK), lambda b_, i: (0, 0))],
        out_specs=pl.BlockSpec((1, tm, GLA_DK), lambda b_, i: (b_, i, 0)),
        out_shape=jax.ShapeDtypeStruct((B, S, GLA_DK), F32),
        compiler_params=_params("parallel", "parallel"),
        name="gla_gate",
    )(x, wg, w2, b.reshape(1, GLA_DK))


def _gla_body(q_ref, k_ref, v_ref, r_ref, b_ref, g_ref, o_ref, st_ref, *, hp):
    C = GLA_CHUNK
    dk, dv = GLA_DK // GLA_HEADS, GLA_DV // GLA_HEADS

    @pl.when(pl.program_id(2) == 0)
    def _():
        st_ref[...] = jnp.zeros_like(st_ref)

    causal = (lax.broadcasted_iota(jnp.int32, (C, C), 0)
              >= lax.broadcasted_iota(jnp.int32, (C, C), 1))
    st = [st_ref[hh] for hh in range(hp)]
    for c in range(q_ref.shape[1] // C):
        sl = slice(c * C, (c + 1) * C)
        for hh in range(hp):
            ck = slice(hh * dk, (hh + 1) * dk)
            cv = slice(hh * dv, (hh + 1) * dv)
            bc = b_ref[0, sl, ck]
            bl = bc[C - 1:C, :]
            kk = k_ref[0, sl, ck]
            q_dec = (q_ref[0, sl, ck] * (dk ** -0.5) * jnp.exp(bc)).astype(BF16)
            k_intra = (kk * jnp.exp(-bc)).astype(BF16)
            k_state = (kk * jnp.exp(bl - bc)).astype(BF16)
            vb = v_ref[0, sl, cv].astype(BF16)
            s = jnp.where(causal, _dot_nt(q_dec, k_intra), 0.0)
            o = (jnp.dot(s.astype(BF16), vb, preferred_element_type=F32)
                 + _dot_nt(q_dec, st[hh].astype(BF16)))
            st[hh] = st[hh] * jnp.exp(bl) + lax.dot_general(
                vb, k_state, (((0,), (0,)), ((), ())), preferred_element_type=F32)
            o = o * lax.rsqrt(jnp.mean(o * o, axis=-1, keepdims=True) + RMS_EPS) * g_ref[...]
            r = r_ref[0, sl, cv]
            o_ref[0, hh, sl, :] = (o * (r * jax.nn.sigmoid(r))).astype(o_ref.dtype)
    for hh in range(hp):
        st_ref[hh] = st[hh]


def _gla(proj, bcum, gnorm_g, *, ts=512, hp=4):
    B, S, _ = proj.shape
    H = GLA_HEADS
    dk, dv = GLA_DK // H, GLA_DV // H
    ts = min(ts, S)
    wk, wv = hp * dk, hp * dv
    nk = GLA_DK // wk
    nv = 2 * GLA_DK // wv
    return pl.pallas_call(
        functools.partial(_gla_body, hp=hp),
        grid=(B, H // hp, S // ts),
        in_specs=[pl.BlockSpec((1, ts, wk), lambda b, h, t: (b, t, h)),
                  pl.BlockSpec((1, ts, wk), lambda b, h, t: (b, t, nk + h)),
                  pl.BlockSpec((1, ts, wv), lambda b, h, t: (b, t, nv + h)),
                  pl.BlockSpec((1, ts, wv), lambda b, h, t: (b, t, nv + H // hp + h)),
                  pl.BlockSpec((1, ts, wk), lambda b, h, t: (b, t, h)),
                  pl.BlockSpec((1, dv), lambda b, h, t: (0, 0))],
        out_specs=pl.BlockSpec((1, hp, ts, dv), lambda b, h, t: (b, h, t, 0)),
        out_shape=jax.ShapeDtypeStruct((B, H, S, dv), BF16),
        scratch_shapes=[pltpu.VMEM((hp, dv, dk), F32)],
        compiler_params=_params("parallel", "parallel", "arbitrary"),
        name="gla",
    )(proj, proj, proj, proj, bcum, gnorm_g.reshape(1, dv))


def _diff_mixer(x, w_in, lq1, lk1, lq2, lk2, subln_g, layer_idx):
    lambda_init = 0.8 - 0.6 * math.exp(-0.3 * layer_idx)
    qkv = _proj(x, w_in.astype(BF16), tn=3072)
    return _diff_attention(qkv, lq1, lk1, lq2, lk2, subln_g, lambda_init)


def _dil_mixer(x, w_in):
    wb = w_in.astype(BF16)
    per_group = 3 * DIL_HEADS * DIL_HEAD_DIM
    qkvs = [_proj(x, wb[:, g * per_group:(g + 1) * per_group], dil=dil,
                  tn=per_group if dil == 1 else per_group // 2)
            for g, (_, dil) in enumerate(DIL_GROUPS)]
    return _dilated_attention(qkvs)


def _gla_mixer(x, w_in, w_gate2, b_gate, gnorm_g):
    n_main = 2 * GLA_DK + 2 * GLA_DV
    wb = w_in.astype(BF16)
    proj = _proj(x, wb[:, :n_main], head_major=False, out_dtype=F32, tn=1536)
    wg = jnp.pad(wb[:, n_main:], ((0, 0), (0, LANES - GLA_GATE_RANK)))
    w2 = jnp.pad(w_gate2.astype(BF16), ((0, LANES - GLA_GATE_RANK), (0, 0)))
    bcum = _gla_gate(x, wg, w2, b_gate)
    return _gla(proj, bcum, gnorm_g)


def _finish_layer(o, x, w_out, g1, b1, w1, w2, g2, b2):
    return _out_mlp(o, w_out.astype(BF16), x, g1, b1, w1.astype(BF16), w2.astype(BF16), g2, b2)


def kernel(x, l0_w_in, l0_lam_q1, l0_lam_k1, l0_lam_q2, l0_lam_k2, l0_subln_g, l0_w_out, l0_ln1_g, l0_ln1_b, l0_w_ff1, l0_w_ff2, l0_ln2_g, l0_ln2_b, l1_w_in, l1_w_out, l1_ln1_g, l1_ln1_b, l1_w_ff1, l1_w_ff2, l1_ln2_g, l1_ln2_b, l2_w_in, l2_w_gate2, l2_b_gate, l2_gnorm_g, l2_w_out, l2_ln1_g, l2_ln1_b, l2_w_ff1, l2_w_ff2, l2_ln2_g, l2_ln2_b, l3_w_in, l3_lam_q1, l3_lam_k1, l3_lam_q2, l3_lam_k2, l3_subln_g, l3_w_out, l3_ln1_g, l3_ln1_b, l3_w_ff1, l3_w_ff2, l3_ln2_g, l3_ln2_b):
    o = _diff_mixer(x, l0_w_in, l0_lam_q1, l0_lam_k1, l0_lam_q2, l0_lam_k2, l0_subln_g, 0)
    x = _finish_layer(o, x, l0_w_out, l0_ln1_g, l0_ln1_b, l0_w_ff1, l0_w_ff2, l0_ln2_g, l0_ln2_b)
    o = _dil_mixer(x, l1_w_in)
    x = _finish_layer(o, x, l1_w_out, l1_ln1_g, l1_ln1_b, l1_w_ff1, l1_w_ff2, l1_ln2_g, l1_ln2_b)
    o = _gla_mixer(x, l2_w_in, l2_w_gate2, l2_b_gate, l2_gnorm_g)
    x = _finish_layer(o, x, l2_w_out, l2_ln1_g, l2_ln1_b, l2_w_ff1, l2_w_ff2, l2_ln2_g, l2_ln2_b)
    o = _diff_mixer(x, l3_w_in, l3_lam_q1, l3_lam_k1, l3_lam_q2, l3_lam_k2, l3_subln_g, 3)
    x = _finish_layer(o, x, l3_w_out, l3_ln1_g, l3_ln1_b, l3_w_ff1, l3_w_ff2, l3_ln2_g, l3_ln2_b)
    return x
```
